```python
import math
import jax, jax.numpy as jnp
from jax import lax
import numpy as np

D_MODEL = 2048
BATCH = 2
SEQ = 8192
DEPTH = 4

N_MIXERS = 2
NORM_EPS = 1e-6

SSD_EXPAND = 2
SSD_D_INNER = SSD_EXPAND * D_MODEL
SSD_HEAD_DIM = 64
SSD_N_HEADS = SSD_D_INNER // SSD_HEAD_DIM
SSD_N_GROUPS = 8
SSD_HEADS_PER_GROUP = SSD_N_HEADS // SSD_N_GROUPS
SSD_D_STATE = 128
SSD_CONV_WIDTH = 4
SSD_CHUNK = 128
SSD_CONV_DIM = SSD_D_INNER + 2 * SSD_N_GROUPS * SSD_D_STATE
SSD_IN_DIM = SSD_D_INNER + SSD_CONV_DIM + SSD_N_HEADS

DA_N_HEADS = 16
DA_HEAD_DIM = D_MODEL // DA_N_HEADS // 2
DA_V_DIM = 2 * DA_HEAD_DIM
Q_BLOCK = 128

D_FF = -(-8 * D_MODEL // (3 * 256)) * 256

kernel_name = "hybrid_ssd_diffattn_sandwich"


def rmsnorm(x, w, eps=NORM_EPS):
    xf = x.astype(jnp.float32)
    out = xf * lax.rsqrt(jnp.mean(xf * xf, axis=-1, keepdims=True) + eps)
    return (out * w.astype(jnp.float32)).astype(x.dtype)


def causal_depthwise_conv(u, w, b):
    out = lax.conv_general_dilated(
        u, w[:, None, :].astype(u.dtype), window_strides=(1,),
        padding=[(SSD_CONV_WIDTH - 1, 0)],
        dimension_numbers=("NWC", "WIO", "NWC"),
        feature_group_count=u.shape[-1])
    return out + b.astype(u.dtype)


def ssd_chunked_scan(xdt, da, bm, cm):
    bsz, s_len, h, p = xdt.shape
    g, r, n, l = SSD_N_GROUPS, SSD_HEADS_PER_GROUP, SSD_D_STATE, SSD_CHUNK
    nc = s_len // l
    xs = jnp.moveaxis(xdt.reshape(bsz, nc, l, g, r, p), 1, 0)
    das = jnp.moveaxis(da.reshape(bsz, nc, l, g, r), 1, 0)
    bs = jnp.moveaxis(bm.reshape(bsz, nc, l, g, n), 1, 0)
    cs = jnp.moveaxis(cm.reshape(bsz, nc, l, g, n), 1, 0)
    causal = jnp.tril(jnp.ones((l, l), dtype=bool))

    def step(state, inp):
        x_c, a_c, b_c, c_c = inp
        a_cum = jnp.moveaxis(jnp.cumsum(a_c, axis=1), 1, -1)
        seg = a_cum[..., :, None] - a_cum[..., None, :]
        decay = jnp.exp(jnp.where(causal, seg, -jnp.inf))
        cb = jnp.einsum('blgn,bsgn->bgls', c_c, b_c)
        y_diag = jnp.einsum('bgls,bgrls,bsgrp->blgrp', cb, decay, x_c)
        y_off = jnp.einsum('blgn,bgrpn,bgrl->blgrp', c_c, state, jnp.exp(a_cum))
        a_last = a_cum[..., -1]
        decay_to_end = jnp.exp(a_last[..., None] - a_cum)
        new_state = (state * jnp.exp(a_last)[..., None, None]
                     + jnp.einsum('bsgn,bgrs,bsgrp->bgrpn', b_c, decay_to_end, x_c))
        return new_state, y_diag + y_off

    init = jnp.zeros((bsz, g, r, p, n), jnp.float32)
    _, ys = lax.scan(step, init, (xs, das, bs, cs))
    return jnp.moveaxis(ys, 0, 1).reshape(bsz, s_len, h, p)


def gated_group_rmsnorm(y, z, w):
    gv = (y.astype(jnp.float32) * jax.nn.silu(z.astype(jnp.float32)))
    shp = gv.shape
    gv = gv.reshape(shp[:-1] + (SSD_N_GROUPS, shp[-1] // SSD_N_GROUPS))
    gv = gv * lax.rsqrt(jnp.mean(gv * gv, axis=-1, keepdims=True) + NORM_EPS)
    return (gv.reshape(shp) * w.astype(jnp.float32)).astype(z.dtype)


def ssd_mixer(u, w_in, conv_w, conv_b, dt_bias, a_log, d_skip, norm_w, w_out):
    bsz, s_len, _ = u.shape
    zxbcdt = u @ w_in
    z = zxbcdt[..., :SSD_D_INNER]
    xbc = zxbcdt[..., SSD_D_INNER:SSD_D_INNER + SSD_CONV_DIM]
    dt = zxbcdt[..., SSD_D_INNER + SSD_CONV_DIM:]
    xbc = jax.nn.silu(causal_depthwise_conv(xbc, conv_w, conv_b))
    gn = SSD_N_GROUPS * SSD_D_STATE
    xs = xbc[..., :SSD_D_INNER].reshape(bsz, s_len, SSD_N_HEADS, SSD_HEAD_DIM).astype(jnp.float32)
    bm = xbc[..., SSD_D_INNER:SSD_D_INNER + gn].reshape(bsz, s_len, SSD_N_GROUPS, SSD_D_STATE).astype(jnp.float32)
    cm = xbc[..., SSD_D_INNER + gn:].reshape(bsz, s_len, SSD_N_GROUPS, SSD_D_STATE).astype(jnp.float32)
    dt = jax.nn.softplus(dt.astype(jnp.float32) + dt_bias.astype(jnp.float32))
    a = -jnp.exp(a_log.astype(jnp.float32))
    y = ssd_chunked_scan(xs * dt[..., None], dt * a, bm, cm)
    y = y + d_skip.astype(jnp.float32)[:, None] * xs
    y = gated_group_rmsnorm(y.reshape(bsz, s_len, SSD_D_INNER), z, norm_w)
    return y @ w_out


def diff_attention(u, w_qkv, lq1, lk1, lq2, lk2, subln_w, w_out, lambda_init):
    bsz, s_len, _ = u.shape
    qkv = u @ w_qkv
    q, k, v = jnp.split(qkv, 3, axis=-1)
    q = q.reshape(bsz, s_len, DA_N_HEADS, 2, DA_HEAD_DIM) * (DA_HEAD_DIM ** -0.5)
    k = k.reshape(bsz, s_len, DA_N_HEADS, 2, DA_HEAD_DIM).transpose(0, 2, 3, 1, 4)
    v = v.reshape(bsz, s_len, DA_N_HEADS, DA_V_DIM).transpose(0, 2, 1, 3)
    lam = (jnp.exp(jnp.sum(lq1.astype(jnp.float32) * lk1.astype(jnp.float32)))
           - jnp.exp(jnp.sum(lq2.astype(jnp.float32) * lk2.astype(jnp.float32)))
           + lambda_init)
    nq = s_len // Q_BLOCK
    qb = q.reshape(bsz, nq, Q_BLOCK, DA_N_HEADS, 2, DA_HEAD_DIM).transpose(1, 0, 3, 4, 2, 5)
    key_pos = jnp.arange(s_len)

    def attend(args):
        idx, q_blk = args
        q_pos = idx * Q_BLOCK + jnp.arange(Q_BLOCK)
        s = jnp.einsum('bhcqd,bhcsd->bhcqs', q_blk, k).astype(jnp.float32)
        s = jnp.where(key_pos[None, :] <= q_pos[:, None], s, -jnp.inf)
        pr = jax.nn.softmax(s, axis=-1)
        pr = pr[:, :, 0] - lam * pr[:, :, 1]
        return jnp.einsum('bhqs,bhsd->bqhd', pr.astype(v.dtype), v)

    o = lax.map(attend, (jnp.arange(nq), qb))
    o = o.transpose(1, 0, 2, 3, 4).reshape(bsz, s_len, DA_N_HEADS, DA_V_DIM)
    o = rmsnorm(o, subln_w) * (1.0 - lambda_init)
    return o.reshape(bsz, s_len, D_MODEL) @ w_out


def swiglu(h, w_gate, w_up, w_down):
    return (jax.nn.silu(h @ w_gate) * (h @ w_up)) @ w_down


def setup_inputs(seed: int = 0) -> dict:
    key = jax.random.key(seed)
    ks = jax.random.split(key, 24)
    n_ssd = (DEPTH + 1) // 2
    n_att = DEPTH // 2
    f32 = jnp.float32

    def nrm(k, shape, scale):
        return jax.random.normal(k, shape, f32) * scale

    def gain(k, shape):
        return 1.0 + 0.02 * jax.random.normal(k, shape, f32)

    dt = jnp.exp(jax.random.uniform(ks[12], (n_ssd, SSD_N_HEADS), f32)
                 * (math.log(0.1) - math.log(0.001)) + math.log(0.001))
    return {
        "x": jax.random.normal(ks[0], (BATCH, SEQ, D_MODEL), f32),
        "norm_mix_pre": gain(ks[1], (DEPTH, D_MODEL)),
        "norm_mix_post": gain(ks[2], (DEPTH, D_MODEL)),
        "norm_ffn_pre": gain(ks[3], (DEPTH, D_MODEL)),
        "norm_ffn_post": gain(ks[4], (DEPTH, D_MODEL)),
        "ffn_w_gate": nrm(ks[5], (DEPTH, D_MODEL, D_FF), D_MODEL ** -0.5),
        "ffn_w_up": nrm(ks[6], (DEPTH, D_MODEL, D_FF), D_MODEL ** -0.5),
        "ffn_w_down": nrm(ks[7], (DEPTH, D_FF, D_MODEL), D_FF ** -0.5),
        "ssd_w_in": nrm(ks[8], (n_ssd, D_MODEL, SSD_IN_DIM), D_MODEL ** -0.5),
        "ssd_conv_w": nrm(ks[9], (n_ssd, SSD_CONV_WIDTH, SSD_CONV_DIM), SSD_CONV_WIDTH ** -0.5),
        "ssd_conv_b": nrm(ks[10], (n_ssd, SSD_CONV_DIM), 0.02),
        "ssd_dt_bias": dt + jnp.log(-jnp.expm1(-dt)),
        "ssd_a_log": jnp.log(jax.random.uniform(ks[11], (n_ssd, SSD_N_HEADS), f32, 1.0, 16.0)),
        "ssd_d": gain(ks[13], (n_ssd, SSD_N_HEADS)),
        "ssd_norm": gain(ks[14], (n_ssd, SSD_D_INNER)),
        "ssd_w_out": nrm(ks[15], (n_ssd, SSD_D_INNER, D_MODEL), SSD_D_INNER ** -0.5),
        "da_w_qkv": nrm(ks[16], (n_att, D_MODEL, 3 * D_MODEL), D_MODEL ** -0.5),
        "da_lambda_q1": nrm(ks[17], (n_att, DA_HEAD_DIM), 0.1),
        "da_lambda_k1": nrm(ks[18], (n_att, DA_HEAD_DIM), 0.1),
        "da_lambda_q2": nrm(ks[19], (n_att, DA_HEAD_DIM), 0.1),
        "da_lambda_k2": nrm(ks[20], (n_att, DA_HEAD_DIM), 0.1),
        "da_subln": gain(ks[21], (n_att, DA_V_DIM)),
        "da_w_out": nrm(ks[22], (n_att, D_MODEL, D_MODEL), D_MODEL ** -0.5),
    }


def reference(x, norm_mix_pre, norm_mix_post, norm_ffn_pre, norm_ffn_post,
              ffn_w_gate, ffn_w_up, ffn_w_down,
              ssd_w_in, ssd_conv_w, ssd_conv_b, ssd_dt_bias, ssd_a_log, ssd_d,
              ssd_norm, ssd_w_out,
              da_w_qkv, da_lambda_q1, da_lambda_k1, da_lambda_q2, da_lambda_k2,
              da_subln, da_w_out):
    for i in range(DEPTH):
        j = i // N_MIXERS
        h = rmsnorm(x, norm_mix_pre[i])
        if i % N_MIXERS == 0:
            m = ssd_mixer(h, ssd_w_in[j], ssd_conv_w[j], ssd_conv_b[j], ssd_dt_bias[j],
                          ssd_a_log[j], ssd_d[j], ssd_norm[j], ssd_w_out[j])
        else:
            lambda_init = 0.8 - 0.6 * math.exp(-0.3 * i)
            m = diff_attention(h, da_w_qkv[j], da_lambda_q1[j], da_lambda_k1[j],
                               da_lambda_q2[j], da_lambda_k2[j], da_subln[j], da_w_out[j],
                               lambda_init)
        x = x + rmsnorm(m, norm_mix_post[i])
        h = rmsnorm(x, norm_ffn_pre[i])
        x = x + rmsnorm(swiglu(h, ffn_w_gate[i], ffn_w_up[i], ffn_w_down[i]), norm_ffn_post[i])
    return x
```

```python
import functools
import math

import jax
import jax.numpy as jnp
from jax import lax
from jax.experimental import pallas as pl
from jax.experimental.pallas import tpu as pltpu

D_MODEL = 2048
DEPTH = 4
NORM_EPS = 1e-6

SSD_D_INNER = 4096
SSD_HEAD_DIM = 64
SSD_N_HEADS = 64
SSD_N_GROUPS = 8
SSD_HEADS_PER_GROUP = SSD_N_HEADS // SSD_N_GROUPS
SSD_D_STATE = 128
SSD_CONV_WIDTH = 4
SSD_CHUNK = 128
SSD_GROUP_COLS = SSD_D_INNER // SSD_N_GROUPS
SSD_BC_DIM = 2 * SSD_N_GROUPS * SSD_D_STATE
SSD_MAIN_DIM = 2 * SSD_D_INNER + SSD_BC_DIM

DA_N_HEADS = 16
DA_HEAD_DIM = 64
DA_V_DIM = 128

D_FF = 5632

LANES = 128
SUBLANES = 8
VMEM_LIMIT = 52 * 1024 * 1024

F32 = jnp.float32
BF16 = jnp.bfloat16


def _params(n_axes, vmem=VMEM_LIMIT):
    return pltpu.CompilerParams(dimension_semantics=("arbitrary",) * n_axes, vmem_limit_bytes=vmem)


def _rms(v, w):
    return v * lax.rsqrt(jnp.mean(v * v, axis=-1, keepdims=True) + NORM_EPS) * w


def _rmsnorm_kernel(x_ref, w_ref, o_ref):
    o_ref[...] = _rms(x_ref[...], w_ref[...]).astype(o_ref.dtype)


def _rmsnorm(x, w, tm=512):
    t, d = x.shape
    tm = min(tm, t)
    return pl.pallas_call(
        _rmsnorm_kernel,
        grid=(t // tm,),
        in_specs=[pl.BlockSpec((tm, d), lambda i: (i, 0)), pl.BlockSpec((1, d), lambda i: (0, 0))],
        out_specs=pl.BlockSpec((tm, d), lambda i: (i, 0)),
        out_shape=jax.ShapeDtypeStruct((t, d), BF16),
        compiler_params=_params(1),
        name="rmsnorm",
    )(x, w.reshape(1, d))


def _matmul_kernel(a_ref, w_ref, o_ref):
    o_ref[...] = jnp.dot(a_ref[...], w_ref[...], preferred_element_type=F32).astype(o_ref.dtype)


def _matmul(a, w, out_dtype, tm=1024, tn=1024, name="matmul"):
    m, k = a.shape
    n = w.shape[1]
    tm, tn = min(tm, m), min(tn, n)
    return pl.pallas_call(
        _matmul_kernel,
        grid=(m // tm, n // tn),
        in_specs=[pl.BlockSpec((tm, k), lambda i, j: (i, 0)), pl.BlockSpec((k, tn), lambda i, j: (0, j))],
        out_specs=pl.BlockSpec((tm, tn), lambda i, j: (i, j)),
        out_shape=jax.ShapeDtypeStruct((m, n), out_dtype),
        compiler_params=_params(2),
        name=name,
    )(a, w)


def _swiglu_kernel(a_ref, wg_ref, wu_ref, o_ref):
    a = a_ref[...]
    g = jnp.dot(a, wg_ref[...], preferred_element_type=F32)
    u = jnp.dot(a, wu_ref[...], preferred_element_type=F32)
    o_ref[...] = (g * jax.nn.sigmoid(g) * u).astype(o_ref.dtype)


def _swiglu(a, wg, wu, tm=1024, tn=512):
    m, k = a.shape
    n = wg.shape[1]
    tm = min(tm, m)
    return pl.pallas_call(
        _swiglu_kernel,
        grid=(m // tm, n // tn),
        in_specs=[
            pl.BlockSpec((tm, k), lambda i, j: (i, 0)),
            pl.BlockSpec((k, tn), lambda i, j: (0, j)),
            pl.BlockSpec((k, tn), lambda i, j: (0, j)),
        ],
        out_specs=pl.BlockSpec((tm, tn), lambda i, j: (i, j)),
        out_shape=jax.ShapeDtypeStruct((m, n), BF16),
        compiler_params=_params(2),
        name="swiglu_gate_up",
    )(a, wg, wu)


def _proj_norm_kernel(a_ref, w_ref, x_ref, pw_ref, nw_ref, *rest, has_next):
    if has_next:
        xo_ref, ho_ref, acc_ref = rest
    else:
        xo_ref, acc_ref = rest
    k = pl.program_id(1)

    @pl.when(k == 0)
    def _():
        acc_ref[...] = jnp.zeros_like(acc_ref)

    acc_ref[...] += jnp.dot(a_ref[...], w_ref[...], preferred_element_type=F32)

    @pl.when(k == pl.num_programs(1) - 1)
    def _():
        xn = x_ref[...] + _rms(acc_ref[...], pw_ref[...])
        xo_ref[...] = xn
        if has_next:
            ho_ref[...] = _rms(xn, nw_ref[...]).astype(ho_ref.dtype)


def _proj_norm(a, w, x, post_w, next_w, tm=512, tk=512):
    m, kdim = a.shape
    d = w.shape[1]
    tm, tk = min(tm, m), min(tk, kdim)
    has_next = next_w is not None
    if next_w is None:
        next_w = post_w
    out_shape = [jax.ShapeDtypeStruct((m, d), F32)]
    out_specs = [pl.BlockSpec((tm, d), lambda i, k: (i, 0))]
    if has_next:
        out_shape.append(jax.ShapeDtypeStruct((m, d), BF16))
        out_specs.append(pl.BlockSpec((tm, d), lambda i, k: (i, 0)))
    res = pl.pallas_call(
        functools.partial(_proj_norm_kernel, has_next=has_next),
        grid=(m // tm, kdim // tk),
        in_specs=[
            pl.BlockSpec((tm, tk), lambda i, k: (i, k)),
            pl.BlockSpec((tk, d), lambda i, k: (k, 0)),
            pl.BlockSpec((tm, d), lambda i, k: (i, 0)),
            pl.BlockSpec((1, d), lambda i, k: (0, 0)),
            pl.BlockSpec((1, d), lambda i, k: (0, 0)),
        ],
        out_specs=out_specs,
        out_shape=out_shape,
        scratch_shapes=[pltpu.VMEM((tm, d), F32)],
        compiler_params=_params(2),
        name="proj_norm_residual",
    )(a, w, x, post_w.reshape(1, d), next_w.reshape(1, d))
    return (res[0], res[1]) if has_next else (res[0], None)


def _ssd_kernel(z_ref, x_ref, bc_ref, dt_ref, cwx_ref, cbx_ref, cwbc_ref, cbbc_ref, dtb_ref, alog_ref,
                dsk_ref, nw_ref, r_ref, o_ref, ux_ref, ubc_ref, st_ref):
    ln = SSD_CHUNK
    pad = SUBLANES
    c = pl.program_id(1)

    @pl.when(c == 0)
    def _():
        ux_ref[0:pad, :] = jnp.zeros((pad, SSD_D_INNER), F32)
        ubc_ref[0:pad, :] = jnp.zeros((pad, SSD_BC_DIM), F32)
        st_ref[...] = jnp.zeros_like(st_ref)

    ux_ref[pad:pad + ln, :] = x_ref[...].astype(F32)
    ubc_ref[pad:pad + ln, :] = bc_ref[...].astype(F32)

    def conv_silu(u_ref, w_ref, b_ref, lo, width):
        acc = b_ref[:, lo:lo + width]
        for k in range(SSD_CONV_WIDTH):
            r0 = pad - (SSD_CONV_WIDTH - 1) + k
            acc = acc + u_ref[r0:r0 + ln, lo:lo + width] * w_ref[k:k + 1, lo:lo + width]
        return acc * jax.nn.sigmoid(acc)

    dt = jax.nn.softplus(dt_ref[...] + dtb_ref[...])
    da = dt * (-jnp.exp(alog_ref[...]))
    row = lax.broadcasted_iota(jnp.int32, (ln, LANES), 0)
    acum = da
    shift = 1
    while shift < ln:
        acum = acum + jnp.where(row >= shift, pltpu.roll(acum, shift, 0), 0.0)
        shift *= 2
    acum_t = acum.T
    a_last = acum[ln - 1:ln, :]
    w_dec = dt * jnp.exp(a_last - acum)
    ea = jnp.exp(acum)

    r = r_ref[...]
    per_head = jnp.concatenate([dt, w_dec, ea], axis=0).astype(BF16)
    per_chan = jnp.dot(per_head, r, preferred_element_type=F32)
    e_last = jnp.exp(a_last)
    e_hi = e_last.astype(BF16)
    e_lo = (e_last - e_hi.astype(F32)).astype(BF16)
    e_rows = jnp.concatenate([e_hi, e_lo, jnp.zeros((pad - 2, LANES), BF16)], axis=0)
    e_chan = jnp.dot(e_rows, r, preferred_element_type=F32)
    st_decay = e_chan[0:1, :] + e_chan[1:2, :]

    tri_r = lax.broadcasted_iota(jnp.int32, (ln, ln), 0)
    tri_c = lax.broadcasted_iota(jnp.int32, (ln, ln), 1)
    causal = tri_r >= tri_c
    lane = lax.broadcasted_iota(jnp.int32, (ln, LANES), 1)
    first_head = lane < SSD_HEAD_DIM

    for g in range(SSD_N_GROUPS):
        lo = g * SSD_GROUP_COLS
        cols = slice(lo, lo + SSD_GROUP_COLS)
        xg = conv_silu(ux_ref, cwx_ref, cbx_ref, lo, SSD_GROUP_COLS)
        bg = conv_silu(ubc_ref, cwbc_ref, cbbc_ref, g * SSD_D_STATE, SSD_D_STATE)
        cg = conv_silu(ubc_ref, cwbc_ref, cbbc_ref, (SSD_N_GROUPS + g) * SSD_D_STATE, SSD_D_STATE)
        bb = bg.astype(BF16)
        cb16 = cg.astype(BF16)
        cbm = lax.dot_general(cb16, bb, (((1,), (1,)), ((), ())), preferred_element_type=F32)
        xdt = (xg * per_chan[0:ln, cols]).astype(BF16)
        xdec = (xg * per_chan[ln:2 * ln, cols]).astype(BF16)

        st_old = st_ref[:, cols]
        y = jnp.dot(cb16, st_old.astype(BF16), preferred_element_type=F32) * per_chan[2 * ln:3 * ln, cols]

        parts = []
        for pr in range(SSD_HEADS_PER_GROUP // 2):
            mats = []
            for h in (g * SSD_HEADS_PER_GROUP + 2 * pr, g * SSD_HEADS_PER_GROUP + 2 * pr + 1):
                seg = acum[:, h:h + 1] - acum_t[h:h + 1, :]
                decay = jnp.exp(jnp.where(causal, seg, -jnp.inf))
                mats.append((cbm * decay).astype(BF16))
            lhs = jnp.concatenate(mats, axis=1)
            xp = xdt[:, pr * LANES:(pr + 1) * LANES]
            zero = jnp.zeros_like(xp)
            rhs = jnp.concatenate([jnp.where(first_head, xp, zero), jnp.where(first_head, zero, xp)], axis=0)
            parts.append(jnp.dot(lhs, rhs, preferred_element_type=F32))
        y = y + jnp.concatenate(parts, axis=1)

        st_ref[:, cols] = st_old * st_decay[:, cols] + lax.dot_general(
            bb, xdec, (((0,), (0,)), ((), ())), preferred_element_type=F32)

        y = y + dsk_ref[:, cols] * xg
        zg = z_ref[:, cols].astype(F32)
        gv = y * (zg * jax.nn.sigmoid(zg))
        o_ref[:, cols] = _rms(gv, nw_ref[:, cols]).astype(o_ref.dtype)

    ux_ref[0:pad, :] = ux_ref[ln:ln + pad, :]
    ubc_ref[0:pad, :] = ubc_ref[ln:ln + pad, :]


def _ssd(zx, dt_raw, conv_w, conv_b, dt_bias, a_log, d_skip, norm_w, bsz, s_len):
    ln = SSD_CHUNK
    nc = s_len // ln
    t = bsz * s_len
    pad_h = LANES - SSD_N_HEADS
    cwx, cwbc = conv_w[:, :SSD_D_INNER], conv_w[:, SSD_D_INNER:]
    cbx, cbbc = conv_b[:SSD_D_INNER].reshape(1, -1), conv_b[SSD_D_INNER:].reshape(1, -1)
    dtb = jnp.pad(dt_bias, (0, pad_h)).reshape(1, LANES)
    alog = jnp.pad(a_log, (0, pad_h)).reshape(1, LANES)
    dsk = jnp.repeat(d_skip, SSD_HEAD_DIM).reshape(1, SSD_D_INNER)
    nw = norm_w.reshape(1, SSD_D_INNER)
    expand = (jnp.arange(SSD_D_INNER)[None, :] // SSD_HEAD_DIM == jnp.arange(LANES)[:, None]).astype(BF16)

    def rows(col):
        return lambda b, c: (b * nc + c, col)

    def const(b, c):
        return (0, 0)

    x_blk = SSD_D_INNER // SSD_D_INNER
    bc_blk = 2 * SSD_D_INNER // SSD_BC_DIM
    return pl.pallas_call(
        _ssd_kernel,
        grid=(bsz, nc),
        in_specs=[
            pl.BlockSpec((ln, SSD_D_INNER), rows(0)),
            pl.BlockSpec((ln, SSD_D_INNER), rows(x_blk)),
            pl.BlockSpec((ln, SSD_BC_DIM), rows(bc_blk)),
            pl.BlockSpec((ln, LANES), rows(0)),
            pl.BlockSpec((SSD_CONV_WIDTH, SSD_D_INNER), const),
            pl.BlockSpec((1, SSD_D_INNER), const),
            pl.BlockSpec((SSD_CONV_WIDTH, SSD_BC_DIM), const),
            pl.BlockSpec((1, SSD_BC_DIM), const),
            pl.BlockSpec((1, LANES), const),
            pl.BlockSpec((1, LANES), const),
            pl.BlockSpec((1, SSD_D_INNER), const),
            pl.BlockSpec((1, SSD_D_INNER), const),
            pl.BlockSpec((LANES, SSD_D_INNER), const),
        ],
        out_specs=pl.BlockSpec((ln, SSD_D_INNER), rows(0)),
        out_shape=jax.ShapeDtypeStruct((t, SSD_D_INNER), BF16),
        scratch_shapes=[
            pltpu.VMEM((ln + SUBLANES, SSD_D_INNER), F32),
            pltpu.VMEM((ln + SUBLANES, SSD_BC_DIM), F32),
            pltpu.VMEM((SSD_D_STATE, SSD_D_INNER), F32),
        ],
        compiler_params=_params(2),
        name="ssd_scan",
    )(zx, zx, zx, dt_raw, cwx, cbx, cwbc, cbbc, dtb, alog, dsk, nw, expand)


def _attn_kernel(q_ref, k_ref, v_ref, lq1_ref, lk1_ref, lq2_ref, lk2_ref, sw_ref, o_ref,
                 qs_ref, m_ref, l_ref, acc_ref, *, tq, lambda_init):
    qi = pl.program_id(2)
    q = q_ref[...] * (DA_HEAD_DIM ** -0.5)
    lane = lax.broadcasted_iota(jnp.int32, (tq, LANES), 1)
    zero = jnp.zeros_like(q)
    qs_ref[0:tq, :] = jnp.where(lane < DA_HEAD_DIM, q, zero)
    qs_ref[tq:2 * tq, :] = jnp.where(lane < DA_HEAD_DIM, zero, q)
    m_ref[...] = jnp.full_like(m_ref, -jnp.inf)
    l_ref[...] = jnp.zeros_like(l_ref)
    acc_ref[...] = jnp.zeros_like(acc_ref)

    def step(j, masked):
        start = pl.multiple_of(j * tq, tq)
        kb = k_ref[pl.ds(start, tq), :]
        vb = v_ref[pl.ds(start, tq), :]
        s = lax.dot_general(qs_ref[...], kb, (((1,), (1,)), ((), ())), preferred_element_type=F32)
        if masked:
            r_i = lax.broadcasted_iota(jnp.int32, (2 * tq, tq), 0) & (tq - 1)
            c_i = lax.broadcasted_iota(jnp.int32, (2 * tq, tq), 1)
            s = jnp.where(c_i <= r_i, s, -jnp.inf)
        m_prev = m_ref[...]
        m_next = jnp.maximum(m_prev, jnp.max(s, axis=-1, keepdims=True))
        p = jnp.exp(s - pltpu.repeat(m_next, tq // LANES, axis=1))
        alpha = jnp.exp(m_prev - m_next)
        l_ref[...] = alpha * l_ref[...] + jnp.sum(p, axis=-1, keepdims=True)
        acc_ref[...] = alpha * acc_ref[...] + jnp.dot(p.astype(vb.dtype), vb, preferred_element_type=F32)
        m_ref[...] = m_next

    def body(j, carry):
        step(j, False)
        return carry

    lax.fori_loop(0, qi, body, 0)
    step(qi, True)

    lam = (jnp.exp(jnp.sum(lq1_ref[...] * lk1_ref[...], axis=-1, keepdims=True))
           - jnp.exp(jnp.sum(lq2_ref[...] * lk2_ref[...], axis=-1, keepdims=True)) + lambda_init)
    o_all = acc_ref[...] / l_ref[...]
    o = o_all[0:tq, :] - lam * o_all[tq:2 * tq, :]
    o_ref[...] = (_rms(o, sw_ref[...]) * (1.0 - lambda_init)).astype(o_ref.dtype)


def _diff_attn(qkv, lq1, lk1, lq2, lk2, subln_w, lambda_init, bsz, s_len, tq=256):
    t = bsz * s_len
    tq = min(tq, s_len)
    nq = s_len // tq
    nh = DA_N_HEADS

    def const(b, h, i):
        return (0, 0)

    lam_spec = pl.BlockSpec((1, DA_HEAD_DIM), const)
    return pl.pallas_call(
        functools.partial(_attn_kernel, tq=tq, lambda_init=lambda_init),
        grid=(bsz, nh, nq),
        in_specs=[
            pl.BlockSpec((tq, DA_V_DIM), lambda b, h, i: (b * nq + i, h)),
            pl.BlockSpec((s_len, DA_V_DIM), lambda b, h, i: (b, nh + h)),
            pl.BlockSpec((s_len, DA_V_DIM), lambda b, h, i: (b, 2 * nh + h)),
            lam_spec, lam_spec, lam_spec, lam_spec,
            pl.BlockSpec((1, DA_V_DIM), const),
        ],
        out_specs=pl.BlockSpec((tq, DA_V_DIM), lambda b, h, i: (b * nq + i, h)),
        out_shape=jax.ShapeDtypeStruct((t, D_MODEL), BF16),
        scratch_shapes=[
            pltpu.VMEM((2 * tq, DA_V_DIM), BF16),
            pltpu.VMEM((2 * tq, LANES), F32),
            pltpu.VMEM((2 * tq, LANES), F32),
            pltpu.VMEM((2 * tq, DA_V_DIM), F32),
        ],
        compiler_params=_params(3),
        name="diff_attention",
    )(qkv, qkv, qkv, lq1.reshape(1, -1), lk1.reshape(1, -1), lq2.reshape(1, -1), lk2.reshape(1, -1),
      subln_w.reshape(1, -1))


def kernel(x, norm_mix_pre, norm_mix_post, norm_ffn_pre, norm_ffn_post, ffn_w_gate, ffn_w_up, ffn_w_down,
           ssd_w_in, ssd_conv_w, ssd_conv_b, ssd_dt_bias, ssd_a_log, ssd_d, ssd_norm, ssd_w_out,
           da_w_qkv, da_lambda_q1, da_lambda_k1, da_lambda_q2, da_lambda_k2, da_subln, da_w_out):
    bsz, s_len, d = x.shape
    t = bsz * s_len
    xf = x.reshape(t, d)
    h = _rmsnorm(xf, norm_mix_pre[0])
    for i in range(DEPTH):
        j = i // 2
        if i % 2 == 0:
            w_in = ssd_w_in[j]
            w_main = w_in[:, :SSD_MAIN_DIM].astype(BF16)
            w_dt = jnp.pad(w_in[:, SSD_MAIN_DIM:], ((0, 0), (0, LANES - SSD_N_HEADS))).astype(BF16)
            zx = _matmul(h, w_main, BF16, name="ssd_in_proj")
            dt_raw = _matmul(h, w_dt, F32, name="ssd_dt_proj")
            mixed = _ssd(zx, dt_raw, ssd_conv_w[j], ssd_conv_b[j], ssd_dt_bias[j], ssd_a_log[j], ssd_d[j],
                         ssd_norm[j], bsz, s_len)
            w_out = ssd_w_out[j]
        else:
            lambda_init = 0.8 - 0.6 * math.exp(-0.3 * i)
            qkv = _matmul(h, da_w_qkv[j].astype(BF16), BF16, name="attn_qkv_proj")
            mixed = _diff_attn(qkv, da_lambda_q1[j], da_lambda_k1[j], da_lambda_q2[j], da_lambda_k2[j],
                               da_subln[j], lambda_init, bsz, s_len)
            w_out = da_w_out[j]
        xf, h = _proj_norm(mixed, w_out.astype(BF16), xf, norm_mix_post[i], norm_ffn_pre[i])
        act = _swiglu(h, ffn_w_gate[i].astype(BF16), ffn_w_up[i].astype(BF16))
        next_w = norm_mix_pre[i + 1] if i + 1 < DEPTH else None
        xf, h = _proj_norm(act, ffn_w_down[i].astype(BF16), xf, norm_ffn_post[i], next_w)
    return xf.reshape(bsz, s_len, d)
```

```python
import functools
import math

import jax
import jax.numpy as jnp
from jax import lax
from jax.experimental import pallas as pl
from jax.experimental.pallas import tpu as pltpu

D_MODEL = 2048
DEPTH = 4
NORM_EPS = 1e-6

SSD_D_INNER = 4096
SSD_HEAD_DIM = 64
SSD_N_HEADS = 64
SSD_N_GROUPS = 8
SSD_HEADS_PER_GROUP = SSD_N_HEADS // SSD_N_GROUPS
SSD_D_STATE = 128
SSD_CONV_WIDTH = 4
SSD_CHUNK = 128
SSD_GROUP_COLS = SSD_D_INNER // SSD_N_GROUPS
SSD_BC_DIM = 2 * SSD_N_GROUPS * SSD_D_STATE
SSD_MAIN_DIM = 2 * SSD_D_INNER + SSD_BC_DIM

DA_N_HEADS = 16
DA_HEAD_DIM = 64
DA_V_DIM = 128

D_FF = 5632

LANES = 128
SUBLANES = 8
VMEM_LIMIT = 52 * 1024 * 1024

F32 = jnp.float32
BF16 = jnp.bfloat16


def _params(n_axes, vmem=VMEM_LIMIT):
    return pltpu.CompilerParams(dimension_semantics=("arbitrary",) * n_axes, vmem_limit_bytes=vmem)


def _rms(v, w):
    return v * lax.rsqrt(jnp.mean(v * v, axis=-1, keepdims=True) + NORM_EPS) * w


def _rmsnorm_kernel(x_ref, w_ref, o_ref):
    o_ref[...] = _rms(x_ref[...], w_ref[...]).astype(o_ref.dtype)


def _rmsnorm(x, w, tm=512):
    t, d = x.shape
    tm = min(tm, t)
    return pl.pallas_call(
        _rmsnorm_kernel,
        grid=(t // tm,),
        in_specs=[pl.BlockSpec((tm, d), lambda i: (i, 0)), pl.BlockSpec((1, d), lambda i: (0, 0))],
        out_specs=pl.BlockSpec((tm, d), lambda i: (i, 0)),
        out_shape=jax.ShapeDtypeStruct((t, d), BF16),
        compiler_params=_params(1),
        name="rmsnorm",
    )(x, w.reshape(1, d))


def _matmul_kernel(a_ref, w_ref, o_ref):
    o_ref[...] = jnp.dot(a_ref[...], w_ref[...], preferred_element_type=F32).astype(o_ref.dtype)


def _matmul_scaled_kernel(a_ref, w_ref, cs_ref, o_ref):
    acc = jnp.dot(a_ref[...], w_ref[...], preferred_element_type=F32)
    o_ref[...] = (acc * cs_ref[...]).astype(o_ref.dtype)


def _matmul(a, w, out_dtype, col_scale=None, tm=1024, tn=1024, name="matmul"):
    m, k = a.shape
    n = w.shape[1]
    tm, tn = min(tm, m), min(tn, n)
    in_specs = [pl.BlockSpec((tm, k), lambda i, j: (i, 0)), pl.BlockSpec((k, tn), lambda i, j: (0, j))]
    args = [a, w]
    if col_scale is not None:
        in_specs.append(pl.BlockSpec((1, tn), lambda i, j: (0, j)))
        args.append(col_scale.reshape(1, n))
    return pl.pallas_call(
        _matmul_kernel if col_scale is None else _matmul_scaled_kernel,
        grid=(m // tm, n // tn),
        in_specs=in_specs,
        out_specs=pl.BlockSpec((tm, tn), lambda i, j: (i, j)),
        out_shape=jax.ShapeDtypeStruct((m, n), out_dtype),
        compiler_params=_params(2),
        name=name,
    )(*args)


def _swiglu_kernel(a_ref, wg_ref, wu_ref, o_ref):
    a = a_ref[...]
    g = jnp.dot(a, wg_ref[...], preferred_element_type=F32)
    u = jnp.dot(a, wu_ref[...], preferred_element_type=F32)
    o_ref[...] = (g * jax.nn.sigmoid(g) * u).astype(o_ref.dtype)


def _swiglu(a, wg, wu, tm=1024, tn=512):
    m, k = a.shape
    n = wg.shape[1]
    tm = min(tm, m)
    return pl.pallas_call(
        _swiglu_kernel,
        grid=(m // tm, n // tn),
        in_specs=[
            pl.BlockSpec((tm, k), lambda i, j: (i, 0)),
            pl.BlockSpec((k, tn), lambda i, j: (0, j)),
            pl.BlockSpec((k, tn), lambda i, j: (0, j)),
        ],
        out_specs=pl.BlockSpec((tm, tn), lambda i, j: (i, j)),
        out_shape=jax.ShapeDtypeStruct((m, n), BF16),
        compiler_params=_params(2),
        name="swiglu_gate_up",
    )(a, wg, wu)


def _proj_norm_kernel(a_ref, w_ref, x_ref, pw_ref, nw_ref, *rest, has_next, nk):
    if has_next:
        xo_ref, ho_ref, acc_ref = rest
    else:
        xo_ref, acc_ref = rest
    k = pl.program_id(1)
    part = jnp.dot(a_ref[...], w_ref[...], preferred_element_type=F32)

    def finish(total):
        xn = x_ref[...] + _rms(total, pw_ref[...])
        xo_ref[...] = xn
        if has_next:
            ho_ref[...] = _rms(xn, nw_ref[...]).astype(ho_ref.dtype)

    if nk == 1:
        finish(part)
        return

    @pl.when(k == 0)
    def _():
        acc_ref[...] = part

    @pl.when(jnp.logical_and(k > 0, k < nk - 1))
    def _():
        acc_ref[...] += part

    @pl.when(k == nk - 1)
    def _():
        finish(acc_ref[...] + part)


def _proj_norm_tk(kdim):
    nk = 1
    while kdim // nk > 1536 or kdim % nk or (kdim // nk) % LANES:
        nk += 1
    return kdim // nk


def _proj_norm(a, w, x, post_w, next_w, tm=512):
    m, kdim = a.shape
    d = w.shape[1]
    tm = min(tm, m)
    tk = _proj_norm_tk(kdim)
    nk = kdim // tk
    has_next = next_w is not None
    if next_w is None:
        next_w = post_w
    out_shape = [jax.ShapeDtypeStruct((m, d), F32)]
    out_specs = [pl.BlockSpec((tm, d), lambda i, k: (i, 0))]
    if has_next:
        out_shape.append(jax.ShapeDtypeStruct((m, d), BF16))
        out_specs.append(pl.BlockSpec((tm, d), lambda i, k: (i, 0)))
    res = pl.pallas_call(
        functools.partial(_proj_norm_kernel, has_next=has_next, nk=nk),
        grid=(m // tm, nk),
        in_specs=[
            pl.BlockSpec((tm, tk), lambda i, k: (i, k)),
            pl.BlockSpec((tk, d), lambda i, k: (k, 0)),
            pl.BlockSpec((tm, d), lambda i, k: (i, 0)),
            pl.BlockSpec((1, d), lambda i, k: (0, 0)),
            pl.BlockSpec((1, d), lambda i, k: (0, 0)),
        ],
        out_specs=out_specs,
        out_shape=out_shape,
        scratch_shapes=[pltpu.VMEM((tm, d) if nk > 1 else (SUBLANES, LANES), F32)],
        compiler_params=_params(2),
        name="proj_norm_residual",
    )(a, w, x, post_w.reshape(1, d), next_w.reshape(1, d))
    return (res[0], res[1]) if has_next else (res[0], None)


def _ssd_kernel(z_ref, x_ref, bc_ref, dt_ref, cwx_ref, cbx_ref, cwbc_ref, cbbc_ref, dtb_ref, alog_ref,
                dsk_ref, nw_ref, r_ref, o_ref, ux_ref, ubc_ref, st_ref):
    ln = SSD_CHUNK
    pad = SUBLANES
    c = pl.program_id(1)

    @pl.when(c == 0)
    def _():
        ux_ref[0:pad, :] = jnp.zeros((pad, SSD_D_INNER), F32)
        ubc_ref[0:pad, :] = jnp.zeros((pad, SSD_BC_DIM), F32)
        st_ref[...] = jnp.zeros_like(st_ref)

    ux_ref[pad:pad + ln, :] = x_ref[...].astype(F32)
    ubc_ref[pad:pad + ln, :] = bc_ref[...].astype(F32)

    def conv_silu(u_ref, w_ref, b_ref, lo, width):
        acc = b_ref[:, lo:lo + width]
        for k in range(SSD_CONV_WIDTH):
            r0 = pad - (SSD_CONV_WIDTH - 1) + k
            acc = acc + u_ref[r0:r0 + ln, lo:lo + width] * w_ref[k:k + 1, lo:lo + width]
        return acc * jax.nn.sigmoid(acc)

    dt = jax.nn.softplus(dt_ref[...] + dtb_ref[...])
    da = dt * (-jnp.exp(alog_ref[...]))
    row = lax.broadcasted_iota(jnp.int32, (ln, LANES), 0)
    acum = da
    shift = 1
    while shift < ln:
        acum = acum + jnp.where(row >= shift, pltpu.roll(acum, shift, 0), 0.0)
        shift *= 2
    acum_t = acum.T
    a_last = acum[ln - 1:ln, :]
    w_dec = dt * jnp.exp(a_last - acum)
    ea = jnp.exp(acum)

    r = r_ref[...]
    per_head = jnp.concatenate([dt, w_dec, ea], axis=0).astype(BF16)
    per_chan = jnp.dot(per_head, r, preferred_element_type=F32)
    e_last = jnp.exp(a_last)
    e_hi = e_last.astype(BF16)
    e_lo = (e_last - e_hi.astype(F32)).astype(BF16)
    e_rows = jnp.concatenate([e_hi, e_lo, jnp.zeros((pad - 2, LANES), BF16)], axis=0)
    e_chan = jnp.dot(e_rows, r, preferred_element_type=F32)
    st_decay = e_chan[0:1, :] + e_chan[1:2, :]

    tri_r = lax.broadcasted_iota(jnp.int32, (ln, ln), 0)
    tri_c = lax.broadcasted_iota(jnp.int32, (ln, ln), 1)
    causal = tri_r >= tri_c
    lane = lax.broadcasted_iota(jnp.int32, (ln, LANES), 1)
    first_head = lane < SSD_HEAD_DIM

    for g in range(SSD_N_GROUPS):
        lo = g * SSD_GROUP_COLS
        cols = slice(lo, lo + SSD_GROUP_COLS)
        xg = conv_silu(ux_ref, cwx_ref, cbx_ref, lo, SSD_GROUP_COLS)
        bg = conv_silu(ubc_ref, cwbc_ref, cbbc_ref, g * SSD_D_STATE, SSD_D_STATE)
        cg = conv_silu(ubc_ref, cwbc_ref, cbbc_ref, (SSD_N_GROUPS + g) * SSD_D_STATE, SSD_D_STATE)
        bb = bg.astype(BF16)
        cb16 = cg.astype(BF16)
        cbm = lax.dot_general(cb16, bb, (((1,), (1,)), ((), ())), preferred_element_type=F32)
        xdt = (xg * per_chan[0:ln, cols]).astype(BF16)
        xdec = (xg * per_chan[ln:2 * ln, cols]).astype(BF16)

        st_old = st_ref[:, cols]
        y = jnp.dot(cb16, st_old.astype(BF16), preferred_element_type=F32) * per_chan[2 * ln:3 * ln, cols]

        parts = []
        for pr in range(SSD_HEADS_PER_GROUP // 2):
            mats = []
            for h in (g * SSD_HEADS_PER_GROUP + 2 * pr, g * SSD_HEADS_PER_GROUP + 2 * pr + 1):
                seg = acum[:, h:h + 1] - acum_t[h:h + 1, :]
                decay = jnp.exp(jnp.where(causal, seg, -jnp.inf))
                mats.append((cbm * decay).astype(BF16))
            lhs = jnp.concatenate(mats, axis=1)
            xp = xdt[:, pr * LANES:(pr + 1) * LANES]
            zero = jnp.zeros_like(xp)
            rhs = jnp.concatenate([jnp.where(first_head, xp, zero), jnp.where(first_head, zero, xp)], axis=0)
            parts.append(jnp.dot(lhs, rhs, preferred_element_type=F32))
        y = y + jnp.concatenate(parts, axis=1)

        st_ref[:, cols] = st_old * st_decay[:, cols] + lax.dot_general(
            bb, xdec, (((0,), (0,)), ((), ())), preferred_element_type=F32)

        y = y + dsk_ref[:, cols] * xg
        zg = z_ref[:, cols].astype(F32)
        gv = y * (zg * jax.nn.sigmoid(zg))
        o_ref[:, cols] = _rms(gv, nw_ref[:, cols]).astype(o_ref.dtype)

    ux_ref[0:pad, :] = ux_ref[ln:ln + pad, :]
    ubc_ref[0:pad, :] = ubc_ref[ln:ln + pad, :]


def _ssd(zx, dt_raw, conv_w, conv_b, dt_bias, a_log, d_skip, norm_w, bsz, s_len):
    ln = SSD_CHUNK
    nc = s_len // ln
    t = bsz * s_len
    pad_h = LANES - SSD_N_HEADS
    cwx, cwbc = conv_w[:, :SSD_D_INNER], conv_w[:, SSD_D_INNER:]
    cbx, cbbc = conv_b[:SSD_D_INNER].reshape(1, -1), conv_b[SSD_D_INNER:].reshape(1, -1)
    dtb = jnp.pad(dt_bias, (0, pad_h)).reshape(1, LANES)
    alog = jnp.pad(a_log, (0, pad_h)).reshape(1, LANES)
    dsk = jnp.repeat(d_skip, SSD_HEAD_DIM).reshape(1, SSD_D_INNER)
    nw = norm_w.reshape(1, SSD_D_INNER)
    expand = (jnp.arange(SSD_D_INNER)[None, :] // SSD_HEAD_DIM == jnp.arange(LANES)[:, None]).astype(BF16)

    def rows(col):
        return lambda b, c: (b * nc + c, col)

    def const(b, c):
        return (0, 0)

    x_blk = SSD_D_INNER // SSD_D_INNER
    bc_blk = 2 * SSD_D_INNER // SSD_BC_DIM
    return pl.pallas_call(
        _ssd_kernel,
        grid=(bsz, nc),
        in_specs=[
            pl.BlockSpec((ln, SSD_D_INNER), rows(0)),
            pl.BlockSpec((ln, SSD_D_INNER), rows(x_blk)),
            pl.BlockSpec((ln, SSD_BC_DIM), rows(bc_blk)),
            pl.BlockSpec((ln, LANES), rows(0)),
            pl.BlockSpec((SSD_CONV_WIDTH, SSD_D_INNER), const),
            pl.BlockSpec((1, SSD_D_INNER), const),
            pl.BlockSpec((SSD_CONV_WIDTH, SSD_BC_DIM), const),
            pl.BlockSpec((1, SSD_BC_DIM), const),
            pl.BlockSpec((1, LANES), const),
            pl.BlockSpec((1, LANES), const),
            pl.BlockSpec((1, SSD_D_INNER), const),
            pl.BlockSpec((1, SSD_D_INNER), const),
            pl.BlockSpec((LANES, SSD_D_INNER), const),
        ],
        out_specs=pl.BlockSpec((ln, SSD_D_INNER), rows(0)),
        out_shape=jax.ShapeDtypeStruct((t, SSD_D_INNER), BF16),
        scratch_shapes=[
            pltpu.VMEM((ln + SUBLANES, SSD_D_INNER), F32),
            pltpu.VMEM((ln + SUBLANES, SSD_BC_DIM), F32),
            pltpu.VMEM((SSD_D_STATE, SSD_D_INNER), F32),
        ],
        compiler_params=_params(2),
        name="ssd_scan",
    )(zx, zx, zx, dt_raw, cwx, cbx, cwbc, cbbc, dtb, alog, dsk, nw, expand)


def _attn_kernel(q_ref, k_ref, v_ref, lq1_ref, lk1_ref, lq2_ref, lk2_ref, sw_ref, o_ref,
                 qs_ref, s_ref, m_ref, l_ref, acc_ref, *, tq, lambda_init):
    qi = pl.program_id(2)
    nslot = 2 * tq
    q = q_ref[...]
    lane = lax.broadcasted_iota(jnp.int32, (tq, LANES), 1)
    zero = jnp.zeros_like(q)
    qs_ref[0:tq, :] = jnp.where(lane < DA_HEAD_DIM, q, zero)
    qs_ref[tq:nslot, :] = jnp.where(lane < DA_HEAD_DIM, zero, q)
    m_ref[...] = jnp.full_like(m_ref, -jnp.inf)
    l_ref[...] = jnp.zeros_like(l_ref)
    acc_ref[...] = jnp.zeros_like(acc_ref)

    def scores(j):
        start = pl.multiple_of(j * tq, tq)
        kb = k_ref[pl.ds(start, tq), :]
        return lax.dot_general(kb, qs_ref[...], (((1,), (1,)), ((), ())), preferred_element_type=F32)

    def update(s, j, masked):
        start = pl.multiple_of(j * tq, tq)
        vb = v_ref[pl.ds(start, tq), :]
        if masked:
            key = lax.broadcasted_iota(jnp.int32, (tq, nslot), 0)
            qry = lax.broadcasted_iota(jnp.int32, (tq, nslot), 1) & (tq - 1)
            s = jnp.where(key <= qry, s, -jnp.inf)
        m_prev = m_ref[...]
        m_next = jnp.maximum(m_prev, jnp.max(s, axis=0, keepdims=True))
        p = jnp.exp2(s - m_next)
        alpha = jnp.exp2(m_prev - m_next)
        l_ref[...] = alpha * l_ref[...] + jnp.sum(p, axis=0, keepdims=True)
        pv = lax.dot_general(vb, p.astype(vb.dtype), (((0,), (0,)), ((), ())), preferred_element_type=F32)
        acc_ref[...] = alpha * acc_ref[...] + pv
        m_ref[...] = m_next

    s_ref[...] = scores(0)

    def pair(i, carry):
        j = 2 * i
        s_a = s_ref[...]
        s_b = scores(j + 1)
        update(s_a, j, False)
        s_c = scores(j + 2)
        update(s_b, j + 1, False)
        s_ref[...] = s_c
        return carry

    lax.fori_loop(0, qi // 2, pair, 0)

    @pl.when(qi % 2 == 1)
    def _():
        s_a = s_ref[...]
        s_b = scores(qi)
        update(s_a, qi - 1, False)
        update(s_b, qi, True)

    @pl.when(qi % 2 == 0)
    def _():
        update(s_ref[...], qi, True)

    lam = (jnp.exp(jnp.sum(lq1_ref[...] * lk1_ref[...], axis=-1, keepdims=True))
           - jnp.exp(jnp.sum(lq2_ref[...] * lk2_ref[...], axis=-1, keepdims=True)) + lambda_init)
    o_all = acc_ref[...] / l_ref[...]
    o = (o_all[:, 0:tq] - lam * o_all[:, tq:nslot]).T
    o_ref[...] = (_rms(o, sw_ref[...]) * (1.0 - lambda_init)).astype(o_ref.dtype)


def _diff_attn(qkv, lq1, lk1, lq2, lk2, subln_w, lambda_init, bsz, s_len, tq=512):
    t = bsz * s_len
    tq = min(tq, s_len)
    nq = s_len // tq
    nh = DA_N_HEADS

    def const(b, h, i):
        return (0, 0)

    lam_spec = pl.BlockSpec((1, DA_HEAD_DIM), const)
    return pl.pallas_call(
        functools.partial(_attn_kernel, tq=tq, lambda_init=lambda_init),
        grid=(bsz, nh, nq),
        in_specs=[
            pl.BlockSpec((tq, DA_V_DIM), lambda b, h, i: (b * nq + i, h)),
            pl.BlockSpec((s_len, DA_V_DIM), lambda b, h, i: (b, nh + h)),
            pl.BlockSpec((s_len, DA_V_DIM), lambda b, h, i: (b, 2 * nh + h)),
            lam_spec, lam_spec, lam_spec, lam_spec,
            pl.BlockSpec((1, DA_V_DIM), const),
        ],
        out_specs=pl.BlockSpec((tq, DA_V_DIM), lambda b, h, i: (b * nq + i, h)),
        out_shape=jax.ShapeDtypeStruct((t, D_MODEL), BF16),
        scratch_shapes=[
            pltpu.VMEM((2 * tq, DA_V_DIM), BF16),
            pltpu.VMEM((tq, 2 * tq), F32),
            pltpu.VMEM((1, 2 * tq), F32),
            pltpu.VMEM((1, 2 * tq), F32),
            pltpu.VMEM((DA_V_DIM, 2 * tq), F32),
        ],
        compiler_params=_params(3),
        name="diff_attention",
    )(qkv, qkv, qkv, lq1.reshape(1, -1), lk1.reshape(1, -1), lq2.reshape(1, -1), lk2.reshape(1, -1),
      subln_w.reshape(1, -1))


def kernel(x, norm_mix_pre, norm_mix_post, norm_ffn_pre, norm_ffn_post, ffn_w_gate, ffn_w_up, ffn_w_down,
           ssd_w_in, ssd_conv_w, ssd_conv_b, ssd_dt_bias, ssd_a_log, ssd_d, ssd_norm, ssd_w_out,
           da_w_qkv, da_lambda_q1, da_lambda_k1, da_lambda_q2, da_lambda_k2, da_subln, da_w_out):
    bsz, s_len, d = x.shape
    t = bsz * s_len
    xf = x.reshape(t, d)
    h = _rmsnorm(xf, norm_mix_pre[0])
    for i in range(DEPTH):
        j = i // 2
        if i % 2 == 0:
            w_in = ssd_w_in[j]
            w_main = w_in[:, :SSD_MAIN_DIM].astype(BF16)
            w_dt = jnp.pad(w_in[:, SSD_MAIN_DIM:], ((0, 0), (0, LANES - SSD_N_HEADS))).astype(BF16)
            zx = _matmul(h, w_main, BF16, name="ssd_in_proj")
            dt_raw = _matmul(h, w_dt, F32, name="ssd_dt_proj")
            mixed = _ssd(zx, dt_raw, ssd_conv_w[j], ssd_conv_b[j], ssd_dt_bias[j], ssd_a_log[j], ssd_d[j],
                         ssd_norm[j], bsz, s_len)
            w_out = ssd_w_out[j]
        else:
            lambda_init = 0.8 - 0.6 * math.exp(-0.3 * i)
            q_scale = jnp.full((D_MODEL,), DA_HEAD_DIM ** -0.5 * math.log2(math.e), F32)
            qkv_scale = jnp.concatenate([q_scale, jnp.ones((2 * D_MODEL,), F32)])
            qkv = _matmul(h, da_w_qkv[j].astype(BF16), BF16, col_scale=qkv_scale, name="attn_qkv_proj")
            mixed = _diff_attn(qkv, da_lambda_q1[j], da_lambda_k1[j], da_lambda_q2[j], da_lambda_k2[j],
                               da_subln[j], lambda_init, bsz, s_len)
            w_out = da_w_out[j]
        xf, h = _proj_norm(mixed, w_out.astype(BF16), xf, norm_mix_post[i], norm_ffn_pre[i])
        act = _swiglu(h, ffn_w_gate[i].astype(BF16), ffn_w_up[i].astype(BF16))
        next_w = norm_mix_pre[i + 1] if i + 1 < DEPTH else None
        xf, h = _proj_norm(act, ffn_w_down[i].astype(BF16), xf, norm_ffn_post[i], next_w)
    return xf.reshape(bsz, s_len, d)
```

```python
import functools
import math

import jax
import jax.numpy as jnp
from jax import lax
from jax.experimental import pallas as pl
from jax.experimental.pallas import tpu as pltpu

D_MODEL = 2048
DEPTH = 4
NORM_EPS = 1e-6

SSD_D_INNER = 4096
SSD_HEAD_DIM = 64
SSD_N_HEADS = 64
SSD_N_GROUPS = 8
SSD_HEADS_PER_GROUP = SSD_N_HEADS // SSD_N_GROUPS
SSD_D_STATE = 128
SSD_CONV_WIDTH = 4
SSD_CHUNK = 128
SSD_GROUP_COLS = SSD_D_INNER // SSD_N_GROUPS
SSD_BC_DIM = 2 * SSD_N_GROUPS * SSD_D_STATE
SSD_MAIN_DIM = 2 * SSD_D_INNER + SSD_BC_DIM

DA_N_HEADS = 16
DA_HEAD_DIM = 64
DA_V_DIM = 128
ATTN_STRIP = 256

D_FF = 5632

LANES = 128
SUBLANES = 8
VMEM_LIMIT = 52 * 1024 * 1024

F32 = jnp.float32
BF16 = jnp.bfloat16


def _params(n_axes, vmem=VMEM_LIMIT):
    return pltpu.CompilerParams(dimension_semantics=("arbitrary",) * n_axes, vmem_limit_bytes=vmem)


def _rms(v, w):
    return v * lax.rsqrt(jnp.mean(v * v, axis=-1, keepdims=True) + NORM_EPS) * w


def _silu(v):
    return v * (0.5 * jnp.tanh(0.5 * v) + 0.5)


def _rmsnorm_kernel(x_ref, w_ref, o_ref):
    o_ref[...] = _rms(x_ref[...], w_ref[...]).astype(o_ref.dtype)


def _rmsnorm(x, w, tm=512):
    t, d = x.shape
    tm = min(tm, t)
    return pl.pallas_call(
        _rmsnorm_kernel,
        grid=(t // tm,),
        in_specs=[pl.BlockSpec((tm, d), lambda i: (i, 0)), pl.BlockSpec((1, d), lambda i: (0, 0))],
        out_specs=pl.BlockSpec((tm, d), lambda i: (i, 0)),
        out_shape=jax.ShapeDtypeStruct((t, d), BF16),
        compiler_params=_params(1),
        name="rmsnorm",
    )(x, w.reshape(1, d))


def _matmul_kernel(a_ref, w_ref, *rest, scaled):
    if scaled:
        cs_ref, o_ref, wb_ref = rest
    else:
        o_ref, wb_ref = rest

    @pl.when(pl.program_id(1) == 0)
    def _():
        wb_ref[...] = w_ref[...].astype(BF16)

    acc = jnp.dot(a_ref[...], wb_ref[...], preferred_element_type=F32)
    if scaled:
        acc = acc * cs_ref[...]
    o_ref[...] = acc.astype(o_ref.dtype)


def _matmul(a, w, layer, n, out_dtype, col_scale=None, tm=1024, tn=1024, name="matmul"):
    m, k = a.shape
    tm, tn = min(tm, m), min(tn, n)
    in_specs = [pl.BlockSpec((tm, k), lambda j, i: (i, 0)), pl.BlockSpec((None, k, tn), lambda j, i: (layer, 0, j))]
    args = [a, w]
    if col_scale is not None:
        in_specs.append(pl.BlockSpec((1, tn), lambda j, i: (0, j)))
        args.append(col_scale.reshape(1, n))
    return pl.pallas_call(
        functools.partial(_matmul_kernel, scaled=col_scale is not None),
        grid=(n // tn, m // tm),
        in_specs=in_specs,
        out_specs=pl.BlockSpec((tm, tn), lambda j, i: (i, j)),
        out_shape=jax.ShapeDtypeStruct((m, n), out_dtype),
        scratch_shapes=[pltpu.VMEM((k, tn), BF16)],
        compiler_params=_params(2),
        name=name,
    )(*args)


def _swiglu_kernel(a_ref, wg_ref, wu_ref, o_ref, wgb_ref, wub_ref):
    @pl.when(pl.program_id(1) == 0)
    def _():
        wgb_ref[...] = wg_ref[...].astype(BF16)
        wub_ref[...] = wu_ref[...].astype(BF16)

    a = a_ref[...]
    g = jnp.dot(a, wgb_ref[...], preferred_element_type=F32)
    u = jnp.dot(a, wub_ref[...], preferred_element_type=F32)
    o_ref[...] = (g * jax.nn.sigmoid(g) * u).astype(o_ref.dtype)


def _swiglu(a, wg, wu, layer, tm=1024, tn=512):
    m, k = a.shape
    n = wg.shape[2]
    tm = min(tm, m)
    return pl.pallas_call(
        _swiglu_kernel,
        grid=(n // tn, m // tm),
        in_specs=[
            pl.BlockSpec((tm, k), lambda j, i: (i, 0)),
            pl.BlockSpec((None, k, tn), lambda j, i: (layer, 0, j)),
            pl.BlockSpec((None, k, tn), lambda j, i: (layer, 0, j)),
        ],
        out_specs=pl.BlockSpec((tm, tn), lambda j, i: (i, j)),
        out_shape=jax.ShapeDtypeStruct((m, n), BF16),
        scratch_shapes=[pltpu.VMEM((k, tn), BF16), pltpu.VMEM((k, tn), BF16)],
        compiler_params=_params(2),
        name="swiglu_gate_up",
    )(a, wg, wu)


def _proj_norm_kernel(a_ref, w_ref, x_ref, pw_ref, nw_ref, *rest, has_next, nk):
    if has_next:
        xo_ref, ho_ref, acc_ref = rest
    else:
        xo_ref, acc_ref = rest
    k = pl.program_id(1)
    part = jnp.dot(a_ref[...], w_ref[...], preferred_element_type=F32)

    def finish(total):
        xn = x_ref[...] + _rms(total, pw_ref[...])
        xo_ref[...] = xn
        if has_next:
            ho_ref[...] = _rms(xn, nw_ref[...]).astype(ho_ref.dtype)

    if nk == 1:
        finish(part)
        return

    @pl.when(k == 0)
    def _():
        acc_ref[...] = part

    @pl.when(jnp.logical_and(k > 0, k < nk - 1))
    def _():
        acc_ref[...] += part

    @pl.when(k == nk - 1)
    def _():
        finish(acc_ref[...] + part)


def _proj_norm_tk(kdim):
    nk = 1
    while kdim // nk > 1536 or kdim % nk or (kdim // nk) % LANES:
        nk += 1
    return kdim // nk


def _proj_norm(a, w, x, post_w, next_w, tm=512):
    m, kdim = a.shape
    d = w.shape[1]
    tm = min(tm, m)
    tk = _proj_norm_tk(kdim)
    nk = kdim // tk
    has_next = next_w is not None
    if next_w is None:
        next_w = post_w
    out_shape = [jax.ShapeDtypeStruct((m, d), F32)]
    out_specs = [pl.BlockSpec((tm, d), lambda i, k: (i, 0))]
    if has_next:
        out_shape.append(jax.ShapeDtypeStruct((m, d), BF16))
        out_specs.append(pl.BlockSpec((tm, d), lambda i, k: (i, 0)))
    res = pl.pallas_call(
        functools.partial(_proj_norm_kernel, has_next=has_next, nk=nk),
        grid=(m // tm, nk),
        in_specs=[
            pl.BlockSpec((tm, tk), lambda i, k: (i, k)),
            pl.BlockSpec((tk, d), lambda i, k: (k, 0)),
            pl.BlockSpec((tm, d), lambda i, k: (i, 0)),
            pl.BlockSpec((1, d), lambda i, k: (0, 0)),
            pl.BlockSpec((1, d), lambda i, k: (0, 0)),
        ],
        out_specs=out_specs,
        out_shape=out_shape,
        scratch_shapes=[pltpu.VMEM((tm, d) if nk > 1 else (SUBLANES, LANES), F32)],
        compiler_params=_params(2),
        name="proj_norm_residual",
    )(a, w, x, post_w.reshape(1, d), next_w.reshape(1, d))
    return (res[0], res[1]) if has_next else (res[0], None)


def _ssd_kernel(z_ref, x_ref, bc_ref, xp_ref, bcp_ref, dt_ref, cwx_ref, cbx_ref, cwbc_ref, cbbc_ref, dtb_ref,
                alog_ref, dsk_ref, nw_ref, r_ref, o_ref, st_ref):
    ln = SSD_CHUNK
    pad = SUBLANES
    c = pl.program_id(1)
    n_shift = SSD_CONV_WIDTH - 1

    @pl.when(c == 0)
    def _():
        st_ref[...] = jnp.zeros_like(st_ref)

    sh_r = lax.broadcasted_iota(jnp.int32, (n_shift * ln, 2 * ln), 0)
    sh_c = lax.broadcasted_iota(jnp.int32, (n_shift * ln, 2 * ln), 1)
    sh_t = sh_r & (ln - 1)
    sh_k = sh_r // ln
    shift_mat = jnp.where(sh_c == ln + sh_t - (n_shift - sh_k), 1.0, 0.0).astype(BF16)
    has_prev = c > 0

    def conv_taps(cur_ref, prev_ref, lo, width):
        cur = cur_ref[:, lo:lo + width]
        prev = prev_ref[:, lo:lo + width]
        prev = jnp.where(has_prev, prev, jnp.zeros_like(prev))
        stacked = jnp.concatenate([prev, cur], axis=0)
        shifted = jnp.dot(shift_mat, stacked, preferred_element_type=F32)
        return [shifted[k * ln:(k + 1) * ln, :] for k in range(n_shift)] + [cur.astype(F32)]

    def conv_silu(taps, w_ref, b_ref, lo, width, tap_lo=0):
        acc = b_ref[:, lo:lo + width]
        for k in range(SSD_CONV_WIDTH):
            acc = acc + taps[k][:, tap_lo:tap_lo + width] * w_ref[k:k + 1, lo:lo + width]
        return _silu(acc)

    bc_taps = conv_taps(bc_ref, bcp_ref, 0, SSD_BC_DIM)

    dt = jax.nn.softplus(dt_ref[...] + dtb_ref[...])
    da = dt * (-jnp.exp(alog_ref[...]))
    row = lax.broadcasted_iota(jnp.int32, (ln, LANES), 0)
    acum = da
    shift = 1
    while shift < ln:
        acum = acum + jnp.where(row >= shift, pltpu.roll(acum, shift, 0), 0.0)
        shift *= 2
    acum_t = acum.T
    a_last = acum[ln - 1:ln, :]
    w_dec = dt * jnp.exp(a_last - acum)
    ea = jnp.exp(acum)

    r = r_ref[...]
    per_head = jnp.concatenate([dt, w_dec, ea], axis=0).astype(BF16)
    per_chan = jnp.dot(per_head, r, preferred_element_type=F32)
    e_last = jnp.exp(a_last)
    e_hi = e_last.astype(BF16)
    e_lo = (e_last - e_hi.astype(F32)).astype(BF16)
    e_rows = jnp.concatenate([e_hi, e_lo, jnp.zeros((pad - 2, LANES), BF16)], axis=0)
    e_chan = jnp.dot(e_rows, r, preferred_element_type=F32)
    st_decay = e_chan[0:1, :] + e_chan[1:2, :]

    tri_r = lax.broadcasted_iota(jnp.int32, (ln, ln), 0)
    tri_c = lax.broadcasted_iota(jnp.int32, (ln, ln), 1)
    causal = tri_r >= tri_c
    lane = lax.broadcasted_iota(jnp.int32, (ln, LANES), 1)
    first_head = lane < SSD_HEAD_DIM

    for g in range(SSD_N_GROUPS):
        lo = g * SSD_GROUP_COLS
        cols = slice(lo, lo + SSD_GROUP_COLS)
        x_taps = conv_taps(x_ref, xp_ref, lo, SSD_GROUP_COLS)
        xg = conv_silu(x_taps, cwx_ref, cbx_ref, lo, SSD_GROUP_COLS)
        b_lo, c_lo = g * SSD_D_STATE, (SSD_N_GROUPS + g) * SSD_D_STATE
        bg = conv_silu(bc_taps, cwbc_ref, cbbc_ref, b_lo, SSD_D_STATE, tap_lo=b_lo)
        cg = conv_silu(bc_taps, cwbc_ref, cbbc_ref, c_lo, SSD_D_STATE, tap_lo=c_lo)
        bb = bg.astype(BF16)
        cb16 = cg.astype(BF16)
        cbm = lax.dot_general(cb16, bb, (((1,), (1,)), ((), ())), preferred_element_type=F32)
        xdt = (xg * per_chan[0:ln, cols]).astype(BF16)
        xdec = (xg * per_chan[ln:2 * ln, cols]).astype(BF16)

        st_old = st_ref[:, cols]
        y = jnp.dot(cb16, st_old.astype(BF16), preferred_element_type=F32) * per_chan[2 * ln:3 * ln, cols]

        parts = []
        for pr in range(SSD_HEADS_PER_GROUP // 2):
            mats = []
            for h in (g * SSD_HEADS_PER_GROUP + 2 * pr, g * SSD_HEADS_PER_GROUP + 2 * pr + 1):
                seg = acum[:, h:h + 1] - acum_t[h:h + 1, :]
                decay = jnp.exp(jnp.where(causal, seg, -jnp.inf))
                mats.append((cbm * decay).astype(BF16))
            lhs = jnp.concatenate(mats, axis=1)
            xp = xdt[:, pr * LANES:(pr + 1) * LANES]
            zero = jnp.zeros_like(xp)
            rhs = jnp.concatenate([jnp.where(first_head, xp, zero), jnp.where(first_head, zero, xp)], axis=0)
            parts.append(jnp.dot(lhs, rhs, preferred_element_type=F32))
        y = y + jnp.concatenate(parts, axis=1)

        st_ref[:, cols] = st_old * st_decay[:, cols] + lax.dot_general(
            bb, xdec, (((0,), (0,)), ((), ())), preferred_element_type=F32)

        y = y + dsk_ref[:, cols] * xg
        zg = z_ref[:, cols].astype(F32)
        gv = y * _silu(zg)
        o_ref[:, cols] = _rms(gv, nw_ref[:, cols]).astype(o_ref.dtype)


def _ssd(zx, dt_raw, conv_w, conv_b, dt_bias, a_log, d_skip, norm_w, bsz, s_len):
    ln = SSD_CHUNK
    nc = s_len // ln
    t = bsz * s_len
    pad_h = LANES - SSD_N_HEADS
    cwx, cwbc = conv_w[:, :SSD_D_INNER], conv_w[:, SSD_D_INNER:]
    cbx, cbbc = conv_b[:SSD_D_INNER].reshape(1, -1), conv_b[SSD_D_INNER:].reshape(1, -1)
    dtb = jnp.pad(dt_bias, (0, pad_h)).reshape(1, LANES)
    alog = jnp.pad(a_log, (0, pad_h)).reshape(1, LANES)
    dsk = jnp.repeat(d_skip, SSD_HEAD_DIM).reshape(1, SSD_D_INNER)
    nw = norm_w.reshape(1, SSD_D_INNER)
    expand = (jnp.arange(SSD_D_INNER)[None, :] // SSD_HEAD_DIM == jnp.arange(LANES)[:, None]).astype(BF16)

    def rows(col):
        return lambda b, c: (b * nc + c, col)

    def const(b, c):
        return (0, 0)

    def prev_rows(col):
        return lambda b, c: (b * nc + jnp.maximum(c - 1, 0), col)

    x_blk = 1
    bc_blk = 2 * SSD_D_INNER // SSD_BC_DIM
    return pl.pallas_call(
        _ssd_kernel,
        grid=(bsz, nc),
        in_specs=[
            pl.BlockSpec((ln, SSD_D_INNER), rows(0)),
            pl.BlockSpec((ln, SSD_D_INNER), rows(x_blk)),
            pl.BlockSpec((ln, SSD_BC_DIM), rows(bc_blk)),
            pl.BlockSpec((ln, SSD_D_INNER), prev_rows(x_blk)),
            pl.BlockSpec((ln, SSD_BC_DIM), prev_rows(bc_blk)),
            pl.BlockSpec((ln, LANES), rows(0)),
            pl.BlockSpec((SSD_CONV_WIDTH, SSD_D_INNER), const),
            pl.BlockSpec((1, SSD_D_INNER), const),
            pl.BlockSpec((SSD_CONV_WIDTH, SSD_BC_DIM), const),
            pl.BlockSpec((1, SSD_BC_DIM), const),
            pl.BlockSpec((1, LANES), const),
            pl.BlockSpec((1, LANES), const),
            pl.BlockSpec((1, SSD_D_INNER), const),
            pl.BlockSpec((1, SSD_D_INNER), const),
            pl.BlockSpec((LANES, SSD_D_INNER), const),
        ],
        out_specs=pl.BlockSpec((ln, SSD_D_INNER), rows(0)),
        out_shape=jax.ShapeDtypeStruct((t, SSD_D_INNER), BF16),
        scratch_shapes=[pltpu.VMEM((SSD_D_STATE, SSD_D_INNER), F32)],
        compiler_params=_params(2),
        name="ssd_scan",
    )(zx, zx, zx, zx, zx, dt_raw, cwx, cbx, cwbc, cbbc, dtb, alog, dsk, nw, expand)


def _attn_kernel(q_ref, k_ref, v_ref, lq1_ref, lk1_ref, lq2_ref, lk2_ref, sw_ref, o_ref,
                 qs_ref, s_ref, m_ref, l_ref, acc_ref, *, tq, lambda_init):
    qi = pl.program_id(2)
    nslot = 2 * tq
    q = q_ref[...]
    lane = lax.broadcasted_iota(jnp.int32, (tq, LANES), 1)
    zero = jnp.zeros_like(q)
    qs_ref[0:tq, :] = jnp.where(lane < DA_HEAD_DIM, q, zero)
    qs_ref[tq:nslot, :] = jnp.where(lane < DA_HEAD_DIM, zero, q)
    m_ref[...] = jnp.full_like(m_ref, -jnp.inf)
    l_ref[...] = jnp.zeros_like(l_ref)
    acc_ref[...] = jnp.zeros_like(acc_ref)

    def scores(j):
        start = pl.multiple_of(j * tq, tq)
        kb = k_ref[pl.ds(start, tq), :]
        return lax.dot_general(kb, qs_ref[...], (((1,), (1,)), ((), ())), preferred_element_type=F32)

    def update(s_all, j, masked):
        start = pl.multiple_of(j * tq, tq)
        vb = v_ref[pl.ds(start, tq), :]
        m_all, l_all, acc_all = m_ref[...], l_ref[...], acc_ref[...]
        m_out, l_out, acc_out = [], [], []
        for c in range(nslot // ATTN_STRIP):
            sl = slice(c * ATTN_STRIP, (c + 1) * ATTN_STRIP)
            s = s_all[:, sl]
            if masked:
                key = lax.broadcasted_iota(jnp.int32, (tq, ATTN_STRIP), 0)
                qry = (lax.broadcasted_iota(jnp.int32, (tq, ATTN_STRIP), 1) + c * ATTN_STRIP) & (tq - 1)
                s = jnp.where(key <= qry, s, -jnp.inf)
            m_prev = m_all[:, sl]
            m_next = jnp.maximum(m_prev, jnp.max(s, axis=0, keepdims=True))
            p = jnp.exp2(s - m_next)
            alpha = jnp.exp2(m_prev - m_next)
            l_out.append(alpha * l_all[:, sl] + jnp.sum(p, axis=0, keepdims=True))
            pv = lax.dot_general(vb, p.astype(vb.dtype), (((0,), (0,)), ((), ())),
                                 preferred_element_type=F32)
            acc_out.append(alpha * acc_all[:, sl] + pv)
            m_out.append(m_next)
        m_ref[...] = jnp.concatenate(m_out, axis=1)
        l_ref[...] = jnp.concatenate(l_out, axis=1)
        acc_ref[...] = jnp.concatenate(acc_out, axis=1)

    s_ref[...] = scores(0)

    def pair(i, carry):
        j = 2 * i
        s_a = s_ref[...]
        s_b = scores(j + 1)
        s_ref[...] = scores(j + 2)
        update(s_a, j, False)
        update(s_b, j + 1, False)
        return carry

    lax.fori_loop(0, qi // 2, pair, 0)

    @pl.when(qi % 2 == 1)
    def _():
        s_a = s_ref[...]
        s_b = scores(qi)
        update(s_a, qi - 1, False)
        update(s_b, qi, True)

    @pl.when(qi % 2 == 0)
    def _():
        update(s_ref[...], qi, True)

    lam = (jnp.exp(jnp.sum(lq1_ref[...] * lk1_ref[...], axis=-1, keepdims=True))
           - jnp.exp(jnp.sum(lq2_ref[...] * lk2_ref[...], axis=-1, keepdims=True)) + lambda_init)
    o_all = acc_ref[...] / l_ref[...]
    o = (o_all[:, 0:tq] - lam * o_all[:, tq:nslot]).T
    o_ref[...] = (_rms(o, sw_ref[...]) * (1.0 - lambda_init)).astype(o_ref.dtype)


def _diff_attn(qkv, lq1, lk1, lq2, lk2, subln_w, lambda_init, bsz, s_len, tq=512):
    t = bsz * s_len
    tq = min(tq, s_len)
    nq = s_len // tq
    nh = DA_N_HEADS

    def const(b, h, i):
        return (0, 0)

    lam_spec = pl.BlockSpec((1, DA_HEAD_DIM), const)
    return pl.pallas_call(
        functools.partial(_attn_kernel, tq=tq, lambda_init=lambda_init),
        grid=(bsz, nh, nq),
        in_specs=[
            pl.BlockSpec((tq, DA_V_DIM), lambda b, h, i: (b * nq + i, h)),
            pl.BlockSpec((s_len, DA_V_DIM), lambda b, h, i: (b, nh + h)),
            pl.BlockSpec((s_len, DA_V_DIM), lambda b, h, i: (b, 2 * nh + h)),
            lam_spec, lam_spec, lam_spec, lam_spec,
            pl.BlockSpec((1, DA_V_DIM), const),
        ],
        out_specs=pl.BlockSpec((tq, DA_V_DIM), lambda b, h, i: (b * nq + i, h)),
        out_shape=jax.ShapeDtypeStruct((t, D_MODEL), BF16),
        scratch_shapes=[
            pltpu.VMEM((2 * tq, DA_V_DIM), BF16),
            pltpu.VMEM((tq, 2 * tq), F32),
            pltpu.VMEM((1, 2 * tq), F32),
            pltpu.VMEM((1, 2 * tq), F32),
            pltpu.VMEM((DA_V_DIM, 2 * tq), F32),
        ],
        compiler_params=_params(3),
        name="diff_attention",
    )(qkv, qkv, qkv, lq1.reshape(1, -1), lk1.reshape(1, -1), lq2.reshape(1, -1), lk2.reshape(1, -1),
      subln_w.reshape(1, -1))


def kernel(x, norm_mix_pre, norm_mix_post, norm_ffn_pre, norm_ffn_post, ffn_w_gate, ffn_w_up, ffn_w_down,
           ssd_w_in, ssd_conv_w, ssd_conv_b, ssd_dt_bias, ssd_a_log, ssd_d, ssd_norm, ssd_w_out,
           da_w_qkv, da_lambda_q1, da_lambda_k1, da_lambda_q2, da_lambda_k2, da_subln, da_w_out):
    bsz, s_len, d = x.shape
    t = bsz * s_len
    xf = x.reshape(t, d)
    h = _rmsnorm(xf, norm_mix_pre[0])
    for i in range(DEPTH):
        j = i // 2
        if i % 2 == 0:
            w_dt = jnp.pad(ssd_w_in[j, :, SSD_MAIN_DIM:], ((0, 0), (0, LANES - SSD_N_HEADS)))[None]
            zx = _matmul(h, ssd_w_in, j, SSD_MAIN_DIM, BF16, name="ssd_in_proj")
            dt_raw = _matmul(h, w_dt, 0, LANES, F32, name="ssd_dt_proj")
            mixed = _ssd(zx, dt_raw, ssd_conv_w[j], ssd_conv_b[j], ssd_dt_bias[j], ssd_a_log[j], ssd_d[j],
                         ssd_norm[j], bsz, s_len)
            w_out = ssd_w_out[j]
        else:
            lambda_init = 0.8 - 0.6 * math.exp(-0.3 * i)
            q_scale = jnp.full((D_MODEL,), DA_HEAD_DIM ** -0.5 * math.log2(math.e), F32)
            qkv_scale = jnp.concatenate([q_scale, jnp.ones((2 * D_MODEL,), F32)])
            qkv = _matmul(h, da_w_qkv, j, 3 * D_MODEL, BF16, col_scale=qkv_scale, name="attn_qkv_proj")
            mixed = _diff_attn(qkv, da_lambda_q1[j], da_lambda_k1[j], da_lambda_q2[j], da_lambda_k2[j],
                               da_subln[j], lambda_init, bsz, s_len)
            w_out = da_w_out[j]
        xf, h = _proj_norm(mixed, w_out.astype(BF16), xf, norm_mix_post[i], norm_ffn_pre[i])
        act = _swiglu(h, ffn_w_gate, ffn_w_up, i)
        next_w = norm_mix_pre[i + 1] if i + 1 < DEPTH else None
        xf, h = _proj_norm(act, ffn_w_down[i].astype(BF16), xf, norm_ffn_post[i], next_w)
    return xf.reshape(bsz, s_len, d)
```

```python
import functools
import math

import jax
import jax.numpy as jnp
from jax import lax
from jax.experimental import pallas as pl
from jax.experimental.pallas import tpu as pltpu

D_MODEL = 2048
DEPTH = 4
NORM_EPS = 1e-6

SSD_D_INNER = 4096
SSD_HEAD_DIM = 64
SSD_N_HEADS = 64
SSD_N_GROUPS = 8
SSD_HEADS_PER_GROUP = SSD_N_HEADS // SSD_N_GROUPS
SSD_D_STATE = 128
SSD_CONV_WIDTH = 4
SSD_CHUNK = 128
SSD_GROUP_COLS = SSD_D_INNER // SSD_N_GROUPS
SSD_BC_DIM = 2 * SSD_N_GROUPS * SSD_D_STATE
SSD_MAIN_DIM = 2 * SSD_D_INNER + SSD_BC_DIM

DA_N_HEADS = 16
DA_HEAD_DIM = 64
DA_V_DIM = 128
ATTN_STRIP = 512

D_FF = 5632

LANES = 128
SUBLANES = 8
VMEM_LIMIT = 52 * 1024 * 1024

F32 = jnp.float32
BF16 = jnp.bfloat16


def _params(n_axes, vmem=VMEM_LIMIT):
    return pltpu.CompilerParams(dimension_semantics=("arbitrary",) * n_axes, vmem_limit_bytes=vmem)


def _rms(v, w):
    return v * lax.rsqrt(jnp.mean(v * v, axis=-1, keepdims=True) + NORM_EPS) * w


def _silu(v):
    return v * (0.5 * jnp.tanh(0.5 * v) + 0.5)


def _rmsnorm_kernel(x_ref, w_ref, o_ref):
    o_ref[...] = _rms(x_ref[...], w_ref[...]).astype(o_ref.dtype)


def _rmsnorm(x, w, tm=512):
    t, d = x.shape
    tm = min(tm, t)
    return pl.pallas_call(
        _rmsnorm_kernel,
        grid=(t // tm,),
        in_specs=[pl.BlockSpec((tm, d), lambda i: (i, 0)), pl.BlockSpec((1, d), lambda i: (0, 0))],
        out_specs=pl.BlockSpec((tm, d), lambda i: (i, 0)),
        out_shape=jax.ShapeDtypeStruct((t, d), BF16),
        compiler_params=_params(1),
        name="rmsnorm",
    )(x, w.reshape(1, d))


def _matmul_kernel(a_ref, w_ref, *rest, scaled):
    if scaled:
        cs_ref, o_ref, wb_ref = rest
    else:
        o_ref, wb_ref = rest

    @pl.when(pl.program_id(1) == 0)
    def _():
        wb_ref[...] = w_ref[...].astype(BF16)

    acc = jnp.dot(a_ref[...], wb_ref[...], preferred_element_type=F32)
    if scaled:
        acc = acc * cs_ref[...]
    o_ref[...] = acc.astype(o_ref.dtype)


def _matmul(a, w, layer, n, out_dtype, col_scale=None, tm=1024, tn=1024, name="matmul"):
    m, k = a.shape
    tm, tn = min(tm, m), min(tn, n)
    in_specs = [pl.BlockSpec((tm, k), lambda j, i: (i, 0)), pl.BlockSpec((None, k, tn), lambda j, i: (layer, 0, j))]
    args = [a, w]
    if col_scale is not None:
        in_specs.append(pl.BlockSpec((1, tn), lambda j, i: (0, j)))
        args.append(col_scale.reshape(1, n))
    return pl.pallas_call(
        functools.partial(_matmul_kernel, scaled=col_scale is not None),
        grid=(n // tn, m // tm),
        in_specs=in_specs,
        out_specs=pl.BlockSpec((tm, tn), lambda j, i: (i, j)),
        out_shape=jax.ShapeDtypeStruct((m, n), out_dtype),
        scratch_shapes=[pltpu.VMEM((k, tn), BF16)],
        compiler_params=_params(2),
        name=name,
    )(*args)


def _swiglu_kernel(a_ref, wg_ref, wu_ref, o_ref, wgb_ref, wub_ref):
    @pl.when(pl.program_id(1) == 0)
    def _():
        wgb_ref[...] = wg_ref[...].astype(BF16)
        wub_ref[...] = wu_ref[...].astype(BF16)

    a = a_ref[...]
    g = jnp.dot(a, wgb_ref[...], preferred_element_type=F32)
    u = jnp.dot(a, wub_ref[...], preferred_element_type=F32)
    o_ref[...] = (g * jax.nn.sigmoid(g) * u).astype(o_ref.dtype)


def _swiglu(a, wg, wu, layer, tm=1024, tn=512):
    m, k = a.shape
    n = wg.shape[2]
    tm = min(tm, m)
    return pl.pallas_call(
        _swiglu_kernel,
        grid=(n // tn, m // tm),
        in_specs=[
            pl.BlockSpec((tm, k), lambda j, i: (i, 0)),
            pl.BlockSpec((None, k, tn), lambda j, i: (layer, 0, j)),
            pl.BlockSpec((None, k, tn), lambda j, i: (layer, 0, j)),
        ],
        out_specs=pl.BlockSpec((tm, tn), lambda j, i: (i, j)),
        out_shape=jax.ShapeDtypeStruct((m, n), BF16),
        scratch_shapes=[pltpu.VMEM((k, tn), BF16), pltpu.VMEM((k, tn), BF16)],
        compiler_params=_params(2),
        name="swiglu_gate_up",
    )(a, wg, wu)


def _proj_norm_kernel(a_ref, w_ref, x_ref, pw_ref, nw_ref, *rest, has_next, nk):
    if has_next:
        xo_ref, ho_ref, acc_ref = rest
    else:
        xo_ref, acc_ref = rest
    k = pl.program_id(1)
    part = jnp.dot(a_ref[...], w_ref[...], preferred_element_type=F32)

    def finish(total):
        xn = x_ref[...] + _rms(total, pw_ref[...])
        xo_ref[...] = xn
        if has_next:
            ho_ref[...] = _rms(xn, nw_ref[...]).astype(ho_ref.dtype)

    if nk == 1:
        finish(part)
        return

    @pl.when(k == 0)
    def _():
        acc_ref[...] = part

    @pl.when(jnp.logical_and(k > 0, k < nk - 1))
    def _():
        acc_ref[...] += part

    @pl.when(k == nk - 1)
    def _():
        finish(acc_ref[...] + part)


def _proj_norm_tk(kdim):
    nk = 1
    while kdim // nk > 1536 or kdim % nk or (kdim // nk) % LANES:
        nk += 1
    return kdim // nk


def _proj_norm(a, w, x, post_w, next_w, tm=512):
    m, kdim = a.shape
    d = w.shape[1]
    tm = min(tm, m)
    tk = _proj_norm_tk(kdim)
    nk = kdim // tk
    has_next = next_w is not None
    if next_w is None:
        next_w = post_w
    out_shape = [jax.ShapeDtypeStruct((m, d), F32)]
    out_specs = [pl.BlockSpec((tm, d), lambda i, k: (i, 0))]
    if has_next:
        out_shape.append(jax.ShapeDtypeStruct((m, d), BF16))
        out_specs.append(pl.BlockSpec((tm, d), lambda i, k: (i, 0)))
    res = pl.pallas_call(
        functools.partial(_proj_norm_kernel, has_next=has_next, nk=nk),
        grid=(m // tm, nk),
        in_specs=[
            pl.BlockSpec((tm, tk), lambda i, k: (i, k)),
            pl.BlockSpec((tk, d), lambda i, k: (k, 0)),
            pl.BlockSpec((tm, d), lambda i, k: (i, 0)),
            pl.BlockSpec((1, d), lambda i, k: (0, 0)),
            pl.BlockSpec((1, d), lambda i, k: (0, 0)),
        ],
        out_specs=out_specs,
        out_shape=out_shape,
        scratch_shapes=[pltpu.VMEM((tm, d) if nk > 1 else (SUBLANES, LANES), F32)],
        compiler_params=_params(2),
        name="proj_norm_residual",
    )(a, w, x, post_w.reshape(1, d), next_w.reshape(1, d))
    return (res[0], res[1]) if has_next else (res[0], None)


def _ssd_kernel(z_ref, x_ref, bc_ref, xp_ref, bcp_ref, dt_ref, cwx_ref, cbx_ref, cwbc_ref, cbbc_ref, dtb_ref,
                alog_ref, dsk_ref, nw_ref, r_ref, o_ref, st_ref):
    ln = SSD_CHUNK
    pad = SUBLANES
    c = pl.program_id(1)
    n_shift = SSD_CONV_WIDTH - 1

    @pl.when(c == 0)
    def _():
        st_ref[...] = jnp.zeros_like(st_ref)

    sh_r = lax.broadcasted_iota(jnp.int32, (n_shift * ln, 2 * ln), 0)
    sh_c = lax.broadcasted_iota(jnp.int32, (n_shift * ln, 2 * ln), 1)
    sh_t = sh_r & (ln - 1)
    sh_k = sh_r // ln
    shift_mat = jnp.where(sh_c == ln + sh_t - (n_shift - sh_k), 1.0, 0.0).astype(BF16)
    has_prev = c > 0

    def conv_taps(cur_ref, prev_ref, lo, width):
        cur = cur_ref[:, lo:lo + width]
        prev = prev_ref[:, lo:lo + width]
        prev = jnp.where(has_prev, prev, jnp.zeros_like(prev))
        stacked = jnp.concatenate([prev, cur], axis=0)
        shifted = jnp.dot(shift_mat, stacked, preferred_element_type=F32)
        return [shifted[k * ln:(k + 1) * ln, :] for k in range(n_shift)] + [cur.astype(F32)]

    def conv_silu(taps, w_ref, b_ref, lo, width, tap_lo=0):
        acc = b_ref[:, lo:lo + width]
        for k in range(SSD_CONV_WIDTH):
            acc = acc + taps[k][:, tap_lo:tap_lo + width] * w_ref[k:k + 1, lo:lo + width]
        return _silu(acc)

    bc_taps = conv_taps(bc_ref, bcp_ref, 0, SSD_BC_DIM)

    dt = jax.nn.softplus(dt_ref[...] + dtb_ref[...])
    da = dt * (-jnp.exp(alog_ref[...]))
    row = lax.broadcasted_iota(jnp.int32, (ln, LANES), 0)
    acum = da
    shift = 1
    while shift < ln:
        acum = acum + jnp.where(row >= shift, pltpu.roll(acum, shift, 0), 0.0)
        shift *= 2
    acum_t = acum.T
    a_last = acum[ln - 1:ln, :]
    w_dec = dt * jnp.exp(a_last - acum)
    ea = jnp.exp(acum)

    r = r_ref[...]
    per_head = jnp.concatenate([dt, w_dec, ea], axis=0).astype(BF16)
    per_chan = jnp.dot(per_head, r, preferred_element_type=F32)
    e_last = jnp.exp(a_last)
    e_hi = e_last.astype(BF16)
    e_lo = (e_last - e_hi.astype(F32)).astype(BF16)
    e_rows = jnp.concatenate([e_hi, e_lo, jnp.zeros((pad - 2, LANES), BF16)], axis=0)
    e_chan = jnp.dot(e_rows, r, preferred_element_type=F32)
    st_decay = e_chan[0:1, :] + e_chan[1:2, :]

    tri_r = lax.broadcasted_iota(jnp.int32, (ln, ln), 0)
    tri_c = lax.broadcasted_iota(jnp.int32, (ln, ln), 1)
    causal = tri_r >= tri_c
    lane = lax.broadcasted_iota(jnp.int32, (ln, LANES), 1)
    first_head = lane < SSD_HEAD_DIM

    for g in range(SSD_N_GROUPS):
        lo = g * SSD_GROUP_COLS
        cols = slice(lo, lo + SSD_GROUP_COLS)
        x_taps = conv_taps(x_ref, xp_ref, lo, SSD_GROUP_COLS)
        xg = conv_silu(x_taps, cwx_ref, cbx_ref, lo, SSD_GROUP_COLS)
        b_lo, c_lo = g * SSD_D_STATE, (SSD_N_GROUPS + g) * SSD_D_STATE
        bg = conv_silu(bc_taps, cwbc_ref, cbbc_ref, b_lo, SSD_D_STATE, tap_lo=b_lo)
        cg = conv_silu(bc_taps, cwbc_ref, cbbc_ref, c_lo, SSD_D_STATE, tap_lo=c_lo)
        bb = bg.astype(BF16)
        cb16 = cg.astype(BF16)
        cbm = lax.dot_general(cb16, bb, (((1,), (1,)), ((), ())), preferred_element_type=F32)
        xdt = (xg * per_chan[0:ln, cols]).astype(BF16)
        xdec = (xg * per_chan[ln:2 * ln, cols]).astype(BF16)

        st_old = st_ref[:, cols]
        y = jnp.dot(cb16, st_old.astype(BF16), preferred_element_type=F32) * per_chan[2 * ln:3 * ln, cols]

        parts = []
        for pr in range(SSD_HEADS_PER_GROUP // 2):
            mats = []
            for h in (g * SSD_HEADS_PER_GROUP + 2 * pr, g * SSD_HEADS_PER_GROUP + 2 * pr + 1):
                seg = acum[:, h:h + 1] - acum_t[h:h + 1, :]
                decay = jnp.exp(jnp.where(causal, seg, -jnp.inf))
                mats.append((cbm * decay).astype(BF16))
            lhs = jnp.concatenate(mats, axis=1)
            xp = xdt[:, pr * LANES:(pr + 1) * LANES]
            zero = jnp.zeros_like(xp)
            rhs = jnp.concatenate([jnp.where(first_head, xp, zero), jnp.where(first_head, zero, xp)], axis=0)
            parts.append(jnp.dot(lhs, rhs, preferred_element_type=F32))
        y = y + jnp.concatenate(parts, axis=1)

        st_ref[:, cols] = st_old * st_decay[:, cols] + lax.dot_general(
            bb, xdec, (((0,), (0,)), ((), ())), preferred_element_type=F32)

        y = y + dsk_ref[:, cols] * xg
        zg = z_ref[:, cols].astype(F32)
        gv = y * _silu(zg)
        o_ref[:, cols] = _rms(gv, nw_ref[:, cols]).astype(o_ref.dtype)


def _ssd(zx, dt_raw, conv_w, conv_b, dt_bias, a_log, d_skip, norm_w, bsz, s_len):
    ln = SSD_CHUNK
    nc = s_len // ln
    t = bsz * s_len
    pad_h = LANES - SSD_N_HEADS
    cwx, cwbc = conv_w[:, :SSD_D_INNER], conv_w[:, SSD_D_INNER:]
    cbx, cbbc = conv_b[:SSD_D_INNER].reshape(1, -1), conv_b[SSD_D_INNER:].reshape(1, -1)
    dtb = jnp.pad(dt_bias, (0, pad_h)).reshape(1, LANES)
    alog = jnp.pad(a_log, (0, pad_h)).reshape(1, LANES)
    dsk = jnp.repeat(d_skip, SSD_HEAD_DIM).reshape(1, SSD_D_INNER)
    nw = norm_w.reshape(1, SSD_D_INNER)
    expand = (jnp.arange(SSD_D_INNER)[None, :] // SSD_HEAD_DIM == jnp.arange(LANES)[:, None]).astype(BF16)

    def rows(col):
        return lambda b, c: (b * nc + c, col)

    def const(b, c):
        return (0, 0)

    def prev_rows(col):
        return lambda b, c: (b * nc + jnp.maximum(c - 1, 0), col)

    x_blk = 1
    bc_blk = 2 * SSD_D_INNER // SSD_BC_DIM
    return pl.pallas_call(
        _ssd_kernel,
        grid=(bsz, nc),
        in_specs=[
            pl.BlockSpec((ln, SSD_D_INNER), rows(0)),
            pl.BlockSpec((ln, SSD_D_INNER), rows(x_blk)),
            pl.BlockSpec((ln, SSD_BC_DIM), rows(bc_blk)),
            pl.BlockSpec((ln, SSD_D_INNER), prev_rows(x_blk)),
            pl.BlockSpec((ln, SSD_BC_DIM), prev_rows(bc_blk)),
            pl.BlockSpec((ln, LANES), rows(0)),
            pl.BlockSpec((SSD_CONV_WIDTH, SSD_D_INNER), const),
            pl.BlockSpec((1, SSD_D_INNER), const),
            pl.BlockSpec((SSD_CONV_WIDTH, SSD_BC_DIM), const),
            pl.BlockSpec((1, SSD_BC_DIM), const),
            pl.BlockSpec((1, LANES), const),
            pl.BlockSpec((1, LANES), const),
            pl.BlockSpec((1, SSD_D_INNER), const),
            pl.BlockSpec((1, SSD_D_INNER), const),
            pl.BlockSpec((LANES, SSD_D_INNER), const),
        ],
        out_specs=pl.BlockSpec((ln, SSD_D_INNER), rows(0)),
        out_shape=jax.ShapeDtypeStruct((t, SSD_D_INNER), BF16),
        scratch_shapes=[pltpu.VMEM((SSD_D_STATE, SSD_D_INNER), F32)],
        compiler_params=_params(2),
        name="ssd_scan",
    )(zx, zx, zx, zx, zx, dt_raw, cwx, cbx, cwbc, cbbc, dtb, alog, dsk, nw, expand)


def _attn_kernel(q_ref, k_ref, v_ref, lq1_ref, lk1_ref, lq2_ref, lk2_ref, sw_ref, o_ref,
                 qs_ref, s_ref, m_ref, l_ref, acc_ref, *, tq, lambda_init):
    qi = pl.program_id(2)
    nslot = 2 * tq
    q = q_ref[...]
    lane = lax.broadcasted_iota(jnp.int32, (tq, LANES), 1)
    zero = jnp.zeros_like(q)
    qs_ref[0:tq, :] = jnp.where(lane < DA_HEAD_DIM, q, zero)
    qs_ref[tq:nslot, :] = jnp.where(lane < DA_HEAD_DIM, zero, q)
    m_ref[...] = jnp.full_like(m_ref, -jnp.inf)
    l_ref[...] = jnp.zeros_like(l_ref)
    acc_ref[...] = jnp.zeros_like(acc_ref)

    def scores(j):
        start = pl.multiple_of(j * tq, tq)
        kb = k_ref[pl.ds(start, tq), :]
        return lax.dot_general(kb, qs_ref[...], (((1,), (1,)), ((), ())), preferred_element_type=F32)

    def update(s_all, j, masked):
        start = pl.multiple_of(j * tq, tq)
        vb = v_ref[pl.ds(start, tq), :]
        m_all, l_all, acc_all = m_ref[...], l_ref[...], acc_ref[...]
        m_out, l_out, acc_out = [], [], []
        for c in range(nslot // ATTN_STRIP):
            sl = slice(c * ATTN_STRIP, (c + 1) * ATTN_STRIP)
            s = s_all[:, sl]
            if masked:
                key = lax.broadcasted_iota(jnp.int32, (tq, ATTN_STRIP), 0)
                qry = (lax.broadcasted_iota(jnp.int32, (tq, ATTN_STRIP), 1) + c * ATTN_STRIP) & (tq - 1)
                s = jnp.where(key <= qry, s, -jnp.inf)
            m_prev = m_all[:, sl]
            m_next = jnp.maximum(m_prev, jnp.max(s, axis=0, keepdims=True))
            p = jnp.exp2(s - m_next)
            alpha = jnp.exp2(m_prev - m_next)
            l_out.append(alpha * l_all[:, sl] + jnp.sum(p, axis=0, keepdims=True))
            pv = lax.dot_general(vb, p.astype(vb.dtype), (((0,), (0,)), ((), ())),
                                 preferred_element_type=F32)
            acc_out.append(alpha * acc_all[:, sl] + pv)
            m_out.append(m_next)
        m_ref[...] = jnp.concatenate(m_out, axis=1)
        l_ref[...] = jnp.concatenate(l_out, axis=1)
        acc_ref[...] = jnp.concatenate(acc_out, axis=1)

    s_ref[...] = scores(0)

    def pair(i, carry):
        j = 2 * i
        s_a = s_ref[...]
        s_b = scores(j + 1)
        s_ref[...] = scores(j + 2)
        update(s_a, j, False)
        update(s_b, j + 1, False)
        return carry

    lax.fori_loop(0, qi // 2, pair, 0)

    @pl.when(qi % 2 == 1)
    def _():
        s_a = s_ref[...]
        s_b = scores(qi)
        update(s_a, qi - 1, False)
        update(s_b, qi, True)

    @pl.when(qi % 2 == 0)
    def _():
        update(s_ref[...], qi, True)

    lam = (jnp.exp(jnp.sum(lq1_ref[...] * lk1_ref[...], axis=-1, keepdims=True))
           - jnp.exp(jnp.sum(lq2_ref[...] * lk2_ref[...], axis=-1, keepdims=True)) + lambda_init)
    o_all = acc_ref[...] / l_ref[...]
    o = (o_all[:, 0:tq] - lam * o_all[:, tq:nslot]).T
    o_ref[...] = (_rms(o, sw_ref[...]) * (1.0 - lambda_init)).astype(o_ref.dtype)


def _diff_attn(qkv, lq1, lk1, lq2, lk2, subln_w, lambda_init, bsz, s_len, tq=512):
    t = bsz * s_len
    tq = min(tq, s_len)
    nq = s_len // tq
    nh = DA_N_HEADS

    def const(b, h, i):
        return (0, 0)

    lam_spec = pl.BlockSpec((1, DA_HEAD_DIM), const)
    return pl.pallas_call(
        functools.partial(_attn_kernel, tq=tq, lambda_init=lambda_init),
        grid=(bsz, nh, nq),
        in_specs=[
            pl.BlockSpec((tq, DA_V_DIM), lambda b, h, i: (b * nq + i, h)),
            pl.BlockSpec((s_len, DA_V_DIM), lambda b, h, i: (b, nh + h)),
            pl.BlockSpec((s_len, DA_V_DIM), lambda b, h, i: (b, 2 * nh + h)),
            lam_spec, lam_spec, lam_spec, lam_spec,
            pl.BlockSpec((1, DA_V_DIM), const),
        ],
        out_specs=pl.BlockSpec((tq, DA_V_DIM), lambda b, h, i: (b * nq + i, h)),
        out_shape=jax.ShapeDtypeStruct((t, D_MODEL), BF16),
        scratch_shapes=[
            pltpu.VMEM((2 * tq, DA_V_DIM), BF16),
            pltpu.VMEM((tq, 2 * tq), F32),
            pltpu.VMEM((1, 2 * tq), F32),
            pltpu.VMEM((1, 2 * tq), F32),
            pltpu.VMEM((DA_V_DIM, 2 * tq), F32),
        ],
        compiler_params=_params(3),
        name="diff_attention",
    )(qkv, qkv, qkv, lq1.reshape(1, -1), lk1.reshape(1, -1), lq2.reshape(1, -1), lk2.reshape(1, -1),
      subln_w.reshape(1, -1))


def kernel(x, norm_mix_pre, norm_mix_post, norm_ffn_pre, norm_ffn_post, ffn_w_gate, ffn_w_up, ffn_w_down,
           ssd_w_in, ssd_conv_w, ssd_conv_b, ssd_dt_bias, ssd_a_log, ssd_d, ssd_norm, ssd_w_out,
           da_w_qkv, da_lambda_q1, da_lambda_k1, da_lambda_q2, da_lambda_k2, da_subln, da_w_out):
    bsz, s_len, d = x.shape
    t = bsz * s_len
    xf = x.reshape(t, d)
    h = _rmsnorm(xf, norm_mix_pre[0])
    for i in range(DEPTH):
        j = i // 2
        if i % 2 == 0:
            w_dt = jnp.pad(ssd_w_in[j, :, SSD_MAIN_DIM:], ((0, 0), (0, LANES - SSD_N_HEADS)))[None]
            zx = _matmul(h, ssd_w_in, j, SSD_MAIN_DIM, BF16, name="ssd_in_proj")
            dt_raw = _matmul(h, w_dt, 0, LANES, F32, name="ssd_dt_proj")
            mixed = _ssd(zx, dt_raw, ssd_conv_w[j], ssd_conv_b[j], ssd_dt_bias[j], ssd_a_log[j], ssd_d[j],
                         ssd_norm[j], bsz, s_len)
            w_out = ssd_w_out[j]
        else:
            lambda_init = 0.8 - 0.6 * math.exp(-0.3 * i)
            q_scale = jnp.full((D_MODEL,), DA_HEAD_DIM ** -0.5 * math.log2(math.e), F32)
            qkv_scale = jnp.concatenate([q_scale, jnp.ones((2 * D_MODEL,), F32)])
            qkv = _matmul(h, da_w_qkv, j, 3 * D_MODEL, BF16, col_scale=qkv_scale, name="attn_qkv_proj")
            mixed = _diff_attn(qkv, da_lambda_q1[j], da_lambda_k1[j], da_lambda_q2[j], da_lambda_k2[j],
                               da_subln[j], lambda_init, bsz, s_len)
            w_out = da_w_out[j]
        xf, h = _proj_norm(mixed, w_out.astype(BF16), xf, norm_mix_post[i], norm_ffn_pre[i])
        act = _swiglu(h, ffn_w_gate, ffn_w_up, i)
        next_w = norm_mix_pre[i + 1] if i + 1 < DEPTH else None
        xf, h = _proj_norm(act, ffn_w_down[i].astype(BF16), xf, norm_ffn_post[i], next_w)
    return xf.reshape(bsz, s_len, d)
```
